```python
import math, functools
import jax, jax.numpy as jnp
from jax import lax
import numpy as np

D_MODEL = 1024
BATCH = 2
SEQ = 8192
DEPTH = 4
DEC_BATCH = 32
DEC_SEQ = 8
PAST_LEN = 8192
PAGE_SIZE = 128

HEAD_DIM = 64
RWKV_HEADS = D_MODEL // 128
RWKV_DH = HEAD_DIM
RWKV_W = RWKV_HEADS * RWKV_DH
DECAY_LORA = 64
AAA_LORA = 64
GATE_LORA = 160
RWKV_PROJ_W = 3 * RWKV_W + DECAY_LORA + AAA_LORA + GATE_LORA
RWKV_GN_EPS = 64e-5
ATT_HEADS = D_MODEL // 128
ATT_DH = HEAD_DIM
ATT_W = ATT_HEADS * ATT_DH
IDX_HEADS = D_MODEL // 128
IDX_DIM = HEAD_DIM
TOPK_KEYS = 256
QUERY_BLOCK = 128
REL_BUCKETS = 32
REL_MAX_DIST = 128
RET_HEADS = D_MODEL // 128
RET_DK = HEAD_DIM
RET_DV = 2 * HEAD_DIM
RET_QK_W = RET_HEADS * RET_DK
RET_V_W = RET_HEADS * RET_DV
RET_CHUNK = 128
ROPE_BASE = 10000.0
N_BRANCH = 3
N_EXPERTS = 32
TOP_K = 4
D_FF = D_MODEL
SWIGLU_LIMIT = 7.0
SWIGLU_ALPHA = 1.702
MOE_BLOCK = 128
RMS_EPS = 1e-6
LN_EPS = 1e-5
NEG_INF = -1e30
IN_WIDTHS = (RWKV_PROJ_W, ATT_W, ATT_W, ATT_W, IDX_HEADS * IDX_DIM, IDX_DIM, IDX_HEADS,
             RET_QK_W, RET_QK_W, RET_V_W, RET_V_W, N_BRANCH * D_MODEL)
IN_W = sum(IN_WIDTHS)

kernel_name = 'hybrid_rwkv7_dsa_retention_moe_step'

F32 = jnp.float32


def _split_points(widths):
    return np.cumsum(np.array(widths))[:-1].tolist()


def rms_norm(x, g):
    xf = x.astype(F32)
    y = xf * lax.rsqrt(jnp.mean(xf * xf, axis=-1, keepdims=True) + RMS_EPS)
    return (y * g).astype(x.dtype)


def head_norm(x, eps):
    xf = x.astype(F32)
    mu = jnp.mean(xf, axis=-1, keepdims=True)
    var = jnp.mean(jnp.square(xf - mu), axis=-1, keepdims=True)
    return (xf - mu) * lax.rsqrt(var + eps)


def layer_norm(x, g, b):
    return (head_norm(x, LN_EPS) * g + b).astype(x.dtype)


def adaln(c, w, b):
    m = jax.nn.silu(c) @ w + b
    return jnp.split(m[:, None, :], 6, axis=-1)


def rotary(x, pos):
    half = x.shape[-1] // 2
    inv = ROPE_BASE ** (-jnp.arange(half, dtype=F32) / half)
    ang = pos.astype(F32)[:, None] * inv[None, :]
    cos = jnp.cos(ang)[None, :, None, :]
    sin = jnp.sin(ang)[None, :, None, :]
    x1, x2 = x[..., :half], x[..., half:]
    return jnp.concatenate([x1 * cos - x2 * sin, x1 * sin + x2 * cos], axis=-1)


def t5_bucket(dist):
    max_exact = REL_BUCKETS // 2
    d = jnp.maximum(dist, 0)
    large = max_exact + (jnp.log(jnp.maximum(d, 1).astype(F32) / max_exact)
                         / math.log(REL_MAX_DIST / max_exact) * (REL_BUCKETS - max_exact)).astype(jnp.int32)
    large = jnp.minimum(large, REL_BUCKETS - 1)
    return jnp.where(d < max_exact, d, large)


def rwkv7_recurrence(r, decay, k, v, kk, a, s0):
    def step(S, xs):
        rt, wt, kt, vt, kkt, at = xs
        sa = jnp.einsum('bhij,bhj->bhi', S, -kkt)
        S = S * wt[:, :, None, :] + sa[..., None] * (kkt * at)[:, :, None, :] + vt[..., None] * kt[:, :, None, :]
        return S, jnp.einsum('bhij,bhj->bhi', S, rt)
    xs = tuple(t.swapaxes(0, 1) for t in (r, decay, k, v, kk, a))
    S, out = lax.scan(step, s0, xs)
    return out.swapaxes(0, 1), S


def rwkv_branch(p, prev, s0, lw):
    B, T, _ = p.shape
    dt = p.dtype
    p_prev = jnp.concatenate([prev[:, None, :].astype(dt), p[:, :-1]], axis=1)
    xm = p + (p_prev - p) * lw['rwkv_mu']
    r, k, v, xw, xa, xg = jnp.split(xm, [RWKV_W, 2 * RWKV_W, 3 * RWKV_W, 3 * RWKV_W + DECAY_LORA,
                                         3 * RWKV_W + DECAY_LORA + AAA_LORA], axis=-1)
    w = -jax.nn.softplus(-(lw['rwkv_w0'] + jnp.tanh(xw) @ lw['rwkv_w2']).astype(F32)) - 0.5
    decay = jnp.exp(-jnp.exp(w))
    a = jax.nn.sigmoid((lw['rwkv_a0'] + xa @ lw['rwkv_a2']).astype(F32))
    g = jax.nn.sigmoid(xg) @ lw['rwkv_g2']
    heads = lambda t: t.reshape(B, T, RWKV_HEADS, RWKV_DH)
    hvec = lambda t: t.reshape(RWKV_HEADS, RWKV_DH)
    kk = heads((k * lw['rwkv_k_k']).astype(F32))
    kk = kk / jnp.maximum(jnp.sqrt(jnp.sum(kk * kk, axis=-1, keepdims=True)), 1e-12)
    k = k.astype(F32) * (1.0 + (a - 1.0) * lw['rwkv_k_a'].astype(F32))
    rh, kh, vh = heads(r.astype(F32)), heads(k), heads(v.astype(F32))
    out, s = rwkv7_recurrence(rh, heads(decay), kh, vh, kk, heads(a), s0.astype(F32))
    out = head_norm(out, RWKV_GN_EPS) * hvec(lw['rwkv_ln_g']) + hvec(lw['rwkv_ln_b'])
    out = out + jnp.sum(rh * kh * hvec(lw['rwkv_r_k']), axis=-1, keepdims=True) * vh
    out = out.reshape(B, T, RWKV_W).astype(dt) * g
    return out, p[:, -1], s.astype(s0.dtype)


def index_scores(qi, wi, ki):
    s = jnp.einsum('bthd,bsd->bths', qi, ki, preferred_element_type=F32) * IDX_DIM ** -0.5
    return jnp.einsum('bths,bth->bts', jax.nn.relu(s), wi.astype(F32))


def gather_rows(arr, idx):
    return jax.vmap(lambda a, i: a[i])(arr, idx)


def attend_selected(q, ks, vs, qpos, idx, rel_bias):
    dist = qpos[None, :, None] - idx
    bias = rel_bias[t5_bucket(dist)]
    logits = (jnp.einsum('bthd,btnhd->bthn', q, ks, preferred_element_type=F32) * ATT_DH ** -0.5
              + jnp.swapaxes(bias, -1, -2).astype(F32))
    logits = jnp.where((dist >= 0)[:, :, None, :], logits, NEG_INF)
    p = jax.nn.softmax(logits, axis=-1)
    return jnp.einsum('bthn,btnhd->bthd', p.astype(vs.dtype), vs)


def dsa_prompt(q, k, v, qi, ki, wi, rel_bias):
    B, S = q.shape[:2]
    qb = math.gcd(S, QUERY_BLOCK)
    n_sel = max(1, min(TOPK_KEYS, S // 4))
    key_pos = jnp.arange(S)

    def block(t0):
        qpos = t0 + jnp.arange(qb)
        sl = lambda a: lax.dynamic_slice_in_dim(a, t0, qb, axis=1)
        scores = index_scores(sl(qi), sl(wi), ki)
        scores = jnp.where(key_pos[None, None, :] <= qpos[None, :, None], scores, NEG_INF)
        _, idx = lax.top_k(scores, n_sel)
        return attend_selected(sl(q), gather_rows(k, idx), gather_rows(v, idx), qpos, idx, rel_bias)

    out = lax.map(block, jnp.arange(0, S, qb))
    return out.swapaxes(0, 1).reshape(B, S, ATT_HEADS, ATT_DH)


def gather_paged(pool, new, page_table, idx):
    ps = pool.shape[1]
    past = page_table.shape[1] * ps
    t_new = new.shape[1]
    in_past = idx < past
    pidx = jnp.minimum(idx, past - 1)
    phys = jax.vmap(lambda pt, i: pt[i])(page_table, pidx // ps)
    rows_past = pool[phys, pidx % ps]
    rows_new = gather_rows(new, jnp.clip(idx - past, 0, t_new - 1))
    return jnp.where(in_past[..., None, None], rows_past, rows_new)


def dsa_sample(q, k, v, qi, ki, wi, rel_bias, cache_k_l, cache_v_l, cache_ki_l, page_table):
    DB, T = q.shape[:2]
    past = page_table.shape[1] * cache_k_l.shape[1]
    ki_past = cache_ki_l[page_table].reshape(DB, past, IDX_DIM).astype(ki.dtype)
    ki_all = jnp.concatenate([ki_past, ki], axis=1)
    L = past + T
    qpos = past + jnp.arange(T)
    scores = index_scores(qi, wi, ki_all)
    scores = jnp.where(jnp.arange(L)[None, None, :] <= qpos[None, :, None], scores, NEG_INF)
    n_sel = max(1, min(TOPK_KEYS, L // 4))
    _, idx = lax.top_k(scores, n_sel)
    ks = gather_paged(cache_k_l, k, page_table, idx)
    vs = gather_paged(cache_v_l, v, page_table, idx)
    return attend_selected(q, ks, vs, qpos, idx, rel_bias)


def retention(q, k, v, s0):
    B, T = q.shape[:2]
    C = math.gcd(T, RET_CHUNK)
    n = T // C
    log_g = jnp.log(1.0 - 2.0 ** (-5.0 - jnp.arange(RET_HEADS, dtype=F32)))
    i = jnp.arange(C, dtype=F32)
    rel = i[:, None] - i[None, :]
    dmask = jnp.where(rel[None] >= 0, jnp.exp(rel[None] * log_g[:, None, None]), 0.0)
    q_dec = jnp.exp((i + 1.0)[:, None] * log_g[None, :])
    k_dec = jnp.exp((C - 1.0 - i)[:, None] * log_g[None, :])
    chunk_dec = jnp.exp(C * log_g)
    to_chunks = lambda a: a.reshape(B, n, C, *a.shape[2:]).swapaxes(0, 1)

    def step(S, xs):
        qc, kc, vc = xs
        inner = jnp.einsum('bihd,bjhd->bhij', qc, kc) * dmask
        o = (jnp.einsum('bhij,bjhe->bihe', inner, vc)
             + jnp.einsum('bihd,bhde->bihe', qc, S) * q_dec[None, :, :, None])
        S = S * chunk_dec[None, :, None, None] + jnp.einsum('bjhd,bjhe->bhde', kc * k_dec[None, :, :, None], vc)
        return S, o

    S, o = lax.scan(step, s0, (to_chunks(q), to_chunks(k), to_chunks(v)))
    return o.swapaxes(0, 1).reshape(B, T, RET_HEADS, RET_DV), S


def retention_branch(pq, pk, pv, pg, s0, pos0, gn_g):
    B, T, _ = pq.shape
    dt = pq.dtype
    pos = pos0 + jnp.arange(T)
    q = rotary(pq.reshape(B, T, RET_HEADS, RET_DK).astype(F32), pos)
    k = rotary(pk.reshape(B, T, RET_HEADS, RET_DK).astype(F32), pos) * RET_DK ** -0.5
    v = pv.reshape(B, T, RET_HEADS, RET_DV).astype(F32)
    o, s = retention(q, k, v, s0.astype(F32))
    o = head_norm(o, LN_EPS) * gn_g.reshape(RET_HEADS, RET_DV)
    o = jax.nn.silu(pg) * o.reshape(B, T, RET_V_W).astype(dt)
    return o, s.astype(s0.dtype)


def clamped_swiglu(u):
    x_glu = jnp.minimum(u[..., ::2], SWIGLU_LIMIT)
    x_lin = jnp.clip(u[..., 1::2], -SWIGLU_LIMIT, SWIGLU_LIMIT)
    return x_glu * jax.nn.sigmoid(SWIGLU_ALPHA * x_glu) * (x_lin + 1.0)


def moe(h, router_w, router_b, w1, b1, w2, b2):
    n = h.shape[0]
    a_n = n * TOP_K
    logits = jnp.dot(h, router_w, preferred_element_type=F32) + router_b.astype(F32)
    top_val, top_idx = lax.top_k(logits, TOP_K)
    gate = jax.nn.softmax(top_val, axis=-1)
    e_flat = top_idx.reshape(-1)
    tok_flat = jnp.repeat(jnp.arange(n, dtype=jnp.int32), TOP_K)
    order = jnp.argsort(e_flat)
    e_sorted = e_flat[order]
    counts = jnp.bincount(e_flat, length=N_EXPERTS)
    starts = jnp.cumsum(counts) - counts
    padded = (counts + MOE_BLOCK - 1) // MOE_BLOCK * MOE_BLOCK
    pends = jnp.cumsum(padded)
    pstarts = pends - padded
    dest_sorted = (pstarts[e_sorted] + jnp.arange(a_n) - starts[e_sorted]).astype(jnp.int32)
    dest = jnp.zeros(a_n, jnp.int32).at[order].set(dest_sorted)
    n_blocks = -(-a_n // MOE_BLOCK) + N_EXPERTS
    row_tok = jnp.zeros(n_blocks * MOE_BLOCK, jnp.int32).at[dest].set(tok_flat)
    block_expert = jnp.minimum(jnp.searchsorted(pends, jnp.arange(n_blocks) * MOE_BLOCK, side='right'),
                               N_EXPERTS - 1)
    x_rows = h[row_tok].reshape(n_blocks, MOE_BLOCK, h.shape[-1])

    def expert_block(args):
        xb, e = args
        u = xb @ w1[e] + b1[e]
        return clamped_swiglu(u) @ w2[e] + b2[e]

    y_rows = lax.map(expert_block, (x_rows, block_expert)).reshape(n_blocks * MOE_BLOCK, -1)
    y = y_rows[dest].reshape(n, TOP_K, -1)
    return jnp.einsum('nk,nkd->nd', gate.astype(h.dtype), y)


def layer_group(x, c, lw, rel_bias, pos0, shift0, rwkv0, ret0, attend):
    B, T, _ = x.shape
    sh1, sc1, gt1, sh2, sc2, gt2 = adaln(c, lw['ada_w'], lw['ada_b'])
    h = rms_norm(x, lw['norm1_g']) * (1.0 + sc1) + sh1
    p = h @ lw['w_in']
    (p_rwkv, pq, pk, pv, pqi, pki, pwi, rq, rk, rv, rg, pgate) = jnp.split(p, _split_points(IN_WIDTHS), axis=-1)
    o_a, shift_new, rwkv_new = rwkv_branch(p_rwkv, shift0, rwkv0, lw)
    q = rms_norm(pq.reshape(B, T, ATT_HEADS, ATT_DH), lw['q_norm_g'])
    k = rms_norm(pk.reshape(B, T, ATT_HEADS, ATT_DH), lw['k_norm_g'])
    v = pv.reshape(B, T, ATT_HEADS, ATT_DH)
    qi = pqi.reshape(B, T, IDX_HEADS, IDX_DIM)
    ki = layer_norm(pki, lw['idx_k_norm_g'], lw['idx_k_norm_b'])
    wi = pwi * IDX_HEADS ** -0.5
    o_b = attend(q, k, v, qi, ki, wi, rel_bias).reshape(B, T, ATT_W)
    o_c, ret_new = retention_branch(rq, rk, rv, rg, ret0, pos0, lw['ret_gn_g'])
    g_a, g_b, g_c = jnp.split(jax.nn.sigmoid(pgate), N_BRANCH, axis=-1)
    merged = (g_a * (o_a @ lw['w_up_rwkv']) + g_b * (o_b @ lw['w_up_att'])
              + g_c * (o_c @ lw['w_up_ret']))
    x = x + gt1 * (merged @ lw['w_out'])
    h2 = rms_norm(x, lw['norm2_g']) * (1.0 + sc2) + sh2
    y = moe(h2.reshape(B * T, D_MODEL), lw['router_w'], lw['router_b'], lw['moe_w1'], lw['moe_b1'],
            lw['moe_w2'], lw['moe_b2'])
    x = x + gt2 * y.reshape(B, T, D_MODEL)
    return x, k, v, ki, rwkv_new, shift_new, ret_new


def setup_inputs(seed: int = 0) -> dict:
    key = jax.random.key(seed)
    keys = iter(jax.random.split(key, 48))

    def nrm(shape, scale):
        return jax.random.normal(next(keys), shape, jnp.float32) * scale

    def uni(shape, lo, hi):
        return jax.random.uniform(next(keys), shape, jnp.float32, lo, hi)

    n_pages = PAST_LEN // PAGE_SIZE
    n_used = DEC_BATCH * n_pages
    n_pool = n_used + n_used // 4
    L, D = DEPTH, D_MODEL
    return {
        'x_prompt': nrm((BATCH, SEQ, D), 1.0),
        'x_sample': nrm((DEC_BATCH, DEC_SEQ, D), 1.0),
        'c_prompt': nrm((BATCH, D), 1.0),
        'c_sample': nrm((DEC_BATCH, D), 1.0),
        'cache_k': nrm((L, n_pool, PAGE_SIZE, ATT_HEADS, ATT_DH), 1.0),
        'cache_v': nrm((L, n_pool, PAGE_SIZE, ATT_HEADS, ATT_DH), 1.0),
        'cache_kidx': nrm((L, n_pool, PAGE_SIZE, IDX_DIM), 1.0),
        'state_rwkv': nrm((L, DEC_BATCH, RWKV_HEADS, RWKV_DH, RWKV_DH), 0.1),
        'state_rwkv_shift': nrm((L, DEC_BATCH, RWKV_PROJ_W), 1.0),
        'state_ret': nrm((L, DEC_BATCH, RET_HEADS, RET_DK, RET_DV), 0.1),
        'page_table': jax.random.permutation(next(keys), n_pool)[:n_used].reshape(DEC_BATCH, n_pages).astype(jnp.int32),
        'rel_bias': nrm((REL_BUCKETS, ATT_HEADS), 0.5),
        'ada_w': nrm((L, D, 6 * D), 0.5 * D ** -0.5),
        'ada_b': nrm((L, 6 * D), 0.02),
        'norm1_g': 1.0 + nrm((L, D), 0.05),
        'norm2_g': 1.0 + nrm((L, D), 0.05),
        'w_in': nrm((L, D, IN_W), D ** -0.5),
        'rwkv_mu': uni((L, RWKV_PROJ_W), 0.0, 1.0),
        'rwkv_w0': uni((L, RWKV_W), -4.0, 0.0),
        'rwkv_w2': nrm((L, DECAY_LORA, RWKV_W), 0.1 * DECAY_LORA ** -0.5),
        'rwkv_a0': nrm((L, RWKV_W), 0.1),
        'rwkv_a2': nrm((L, AAA_LORA, RWKV_W), 0.1 * AAA_LORA ** -0.5),
        'rwkv_g2': nrm((L, GATE_LORA, RWKV_W), GATE_LORA ** -0.5),
        'rwkv_k_k': 0.85 + nrm((L, RWKV_W), 0.05),
        'rwkv_k_a': 1.0 + nrm((L, RWKV_W), 0.05),
        'rwkv_r_k': nrm((L, RWKV_W), 0.1),
        'rwkv_ln_g': 1.0 + nrm((L, RWKV_W), 0.05),
        'rwkv_ln_b': nrm((L, RWKV_W), 0.02),
        'q_norm_g': 1.0 + nrm((L, ATT_DH), 0.05),
        'k_norm_g': 1.0 + nrm((L, ATT_DH), 0.05),
        'idx_k_norm_g': 1.0 + nrm((L, IDX_DIM), 0.05),
        'idx_k_norm_b': nrm((L, IDX_DIM), 0.02),
        'ret_gn_g': 1.0 + nrm((L, RET_V_W), 0.05),
        'w_up_rwkv': nrm((L, RWKV_W, D), RWKV_W ** -0.5),
        'w_up_att': nrm((L, ATT_W, D), ATT_W ** -0.5),
        'w_up_ret': nrm((L, RET_V_W, D), RET_V_W ** -0.5),
        'w_out': nrm((L, D, D), D ** -0.5),
        'router_w': nrm((L, D, N_EXPERTS), D ** -0.5),
        'router_b': nrm((L, N_EXPERTS), 0.01),
        'moe_w1': nrm((L, N_EXPERTS, D, 2 * D_FF), D ** -0.5),
        'moe_b1': nrm((L, N_EXPERTS, 2 * D_FF), 0.01),
        'moe_w2': nrm((L, N_EXPERTS, D_FF, D), D_FF ** -0.5),
        'moe_b2': nrm((L, N_EXPERTS, D), 0.01),
    }


def reference(x_prompt, x_sample, c_prompt, c_sample, cache_k, cache_v, cache_kidx, state_rwkv,
              state_rwkv_shift, state_ret, page_table, rel_bias, ada_w, ada_b, norm1_g, norm2_g, w_in,
              rwkv_mu, rwkv_w0, rwkv_w2, rwkv_a0, rwkv_a2, rwkv_g2, rwkv_k_k, rwkv_k_a, rwkv_r_k,
              rwkv_ln_g, rwkv_ln_b, q_norm_g, k_norm_g, idx_k_norm_g, idx_k_norm_b, ret_gn_g,
              w_up_rwkv, w_up_att, w_up_ret, w_out, router_w, router_b, moe_w1, moe_b1, moe_w2, moe_b2):
    past = page_table.shape[1] * cache_k.shape[2]
    bp = x_prompt.shape[0]
    dt = x_prompt.dtype
    xp, xs = x_prompt, x_sample
    new_p = [[], [], [], [], [], []]
    new_s = [[], [], [], [], [], []]
    for l in range(DEPTH):
        lw = dict(ada_w=ada_w[l], ada_b=ada_b[l], norm1_g=norm1_g[l], norm2_g=norm2_g[l], w_in=w_in[l],
                  rwkv_mu=rwkv_mu[l], rwkv_w0=rwkv_w0[l], rwkv_w2=rwkv_w2[l], rwkv_a0=rwkv_a0[l],
                  rwkv_a2=rwkv_a2[l], rwkv_g2=rwkv_g2[l], rwkv_k_k=rwkv_k_k[l], rwkv_k_a=rwkv_k_a[l],
                  rwkv_r_k=rwkv_r_k[l], rwkv_ln_g=rwkv_ln_g[l], rwkv_ln_b=rwkv_ln_b[l],
                  q_norm_g=q_norm_g[l], k_norm_g=k_norm_g[l], idx_k_norm_g=idx_k_norm_g[l],
                  idx_k_norm_b=idx_k_norm_b[l], ret_gn_g=ret_gn_g[l], w_up_rwkv=w_up_rwkv[l],
                  w_up_att=w_up_att[l], w_up_ret=w_up_ret[l], w_out=w_out[l], router_w=router_w[l],
                  router_b=router_b[l], moe_w1=moe_w1[l], moe_b1=moe_b1[l], moe_w2=moe_w2[l],
                  moe_b2=moe_b2[l])
        xp, *st_p = layer_group(xp, c_prompt, lw, rel_bias, 0,
                                jnp.zeros((bp, RWKV_PROJ_W), dt),
                                jnp.zeros((bp, RWKV_HEADS, RWKV_DH, RWKV_DH), dt),
                                jnp.zeros((bp, RET_HEADS, RET_DK, RET_DV), dt),
                                dsa_prompt)
        attend_s = functools.partial(dsa_sample, cache_k_l=cache_k[l], cache_v_l=cache_v[l],
                                     cache_ki_l=cache_kidx[l], page_table=page_table)
        xs, *st_s = layer_group(xs, c_sample, lw, rel_bias, past, state_rwkv_shift[l], state_rwkv[l],
                                state_ret[l], attend_s)
        for lst, a in zip(new_p, st_p):
            lst.append(a)
        for lst, a in zip(new_s, st_s):
            lst.append(a)
    return (xp, xs,
            jnp.stack(new_p[0]), jnp.stack(new_p[1]), jnp.stack(new_p[2]),
            jnp.stack(new_p[3]), jnp.stack(new_p[4]), jnp.stack(new_p[5]),
            jnp.stack(new_s[0]), jnp.stack(new_s[1]), jnp.stack(new_s[2]),
            jnp.stack(new_s[3]), jnp.stack(new_s[4]), jnp.stack(new_s[5]))
```

```python
import functools
import math

import numpy as np
import jax
import jax.numpy as jnp
from jax import lax
from jax.experimental import pallas as pl
from jax.experimental.pallas import tpu as pltpu

F32 = jnp.float32
BF16 = jnp.bfloat16
I32 = jnp.int32

D_MODEL = 1024
HEAD_DIM = 64
N_HEADS = 8
HW = N_HEADS * HEAD_DIM
LANES = 128
DECAY_LORA, AAA_LORA, GATE_LORA = 64, 64, 160
RWKV_PROJ_W = 3 * HW + DECAY_LORA + AAA_LORA + GATE_LORA
RWKV_PAD_W = 1920
LORA_W = RWKV_PAD_W - 3 * HW
RWKV_GN_EPS = 64e-5
TOPK_KEYS = 256
REL_BUCKETS = 32
REL_MAX_DIST = 128
RET_DV = 2 * HEAD_DIM
RET_V_W = N_HEADS * RET_DV
RET_CHUNK = 128
ROPE_BASE = 10000.0
N_EXPERTS = 32
TOP_K = 4
SWIGLU_LIMIT = 7.0
SWIGLU_ALPHA = 1.702
RMS_EPS = 1e-6
LN_EPS = 1e-5
NEG_INF = -1e30
IN_WIDTHS = (RWKV_PROJ_W, HW, HW, HW, HW, HEAD_DIM, N_HEADS, HW, HW, RET_V_W, RET_V_W, 3 * D_MODEL)

C_RWKV, C_KIWI, C_Q, C_K, C_V, C_QI = 0, 1920, 2048, 2560, 3072, 3584
C_RQ, C_RK, C_RV, C_RG, C_GATE, PROJ_W = 4096, 4608, 5120, 6144, 7168, 10240

VMEM_LIMIT = 56 * 1024 * 1024


def _cparams(*sem):
    return pltpu.CompilerParams(dimension_semantics=sem, vmem_limit_bytes=VMEM_LIMIT)


def _bdot(a, b):
    return jnp.dot(a.astype(BF16), b.astype(BF16), preferred_element_type=F32)


def _bdot_nt(a, b):
    return lax.dot_general(a.astype(BF16), b.astype(BF16), (((1,), (1,)), ((), ())),
                           preferred_element_type=F32)


def _hdot(a, b):
    return jnp.dot(a, b, precision=lax.Precision.HIGHEST, preferred_element_type=F32)


def _sigmoid(x):
    return 1.0 / (1.0 + jnp.exp(-x))


def _full(shape):
    n = len(shape)
    return pl.BlockSpec(shape, lambda *_: (0,) * n)


def _ada_kernel(c_ref, w_ref, b_ref, o_ref):
    c = c_ref[...]
    o_ref[...] = _bdot(c * _sigmoid(c), w_ref[...]) + b_ref[...]


def ada_all(c_all, ada_w, ada_b):
    n_l, d, d6 = ada_w.shape
    bp = c_all.shape[0]
    tn = 1024
    return pl.pallas_call(
        _ada_kernel,
        grid=(n_l, d6 // tn),
        in_specs=[pl.BlockSpec((bp, d), lambda l, j: (0, 0)),
                  pl.BlockSpec((None, d, tn), lambda l, j: (l, 0, j)),
                  pl.BlockSpec((None, 1, tn), lambda l, j: (l, 0, j))],
        out_specs=pl.BlockSpec((None, bp, tn), lambda l, j: (l, 0, j)),
        out_shape=jax.ShapeDtypeStruct((n_l, bp, d6), F32),
        compiler_params=_cparams("parallel", "parallel"),
        name="ada",
    )(c_all, ada_w, ada_b.reshape(n_l, 1, d6))


def _mod_rows(m, t_len, tm):
    b, d = m.shape
    if t_len % tm == 0:
        per = t_len // tm
        return m[:, None, :], pl.BlockSpec((None, 1, d), lambda i, *_: (i // per, 0, 0))
    assert tm % t_len == 0 and (b * t_len) % tm == 0
    arr = jnp.repeat(m, t_len, axis=0).reshape(b * t_len // tm, tm, d)
    return arr, pl.BlockSpec((None, tm, d), lambda i, *_: (i, 0, 0))


def _norm_proj_kernel(x_ref, g_ref, sc_ref, sh_ref, w_ref, o_ref, h_scr):
    @pl.when(pl.program_id(1) == 0)
    def _():
        x = x_ref[...]
        y = x * lax.rsqrt(jnp.mean(x * x, axis=-1, keepdims=True) + RMS_EPS) * g_ref[...]
        h_scr[...] = (y * (1.0 + sc_ref[...]) + sh_ref[...]).astype(BF16)

    o_ref[...] = jnp.dot(h_scr[...], w_ref[...], preferred_element_type=F32)


def norm_proj(x2, t_len, g, sc, sh, w):
    n, d = x2.shape
    pw = w.shape[1]
    tm = min(n, 1024)
    tn = 1024
    sc3, sc_spec = _mod_rows(sc, t_len, tm)
    sh3, sh_spec = _mod_rows(sh, t_len, tm)
    return pl.pallas_call(
        _norm_proj_kernel,
        grid=(n // tm, pw // tn),
        in_specs=[pl.BlockSpec((tm, d), lambda i, j: (i, 0)),
                  pl.BlockSpec((1, d), lambda i, j: (0, 0)),
                  sc_spec, sh_spec,
                  pl.BlockSpec((d, tn), lambda i, j: (0, j))],
        out_specs=pl.BlockSpec((tm, tn), lambda i, j: (i, j)),
        out_shape=jax.ShapeDtypeStruct((n, pw), F32),
        scratch_shapes=[pltpu.VMEM((tm, d), BF16)],
        compiler_params=_cparams("parallel", "arbitrary"),
        name="norm_proj",
    )(x2, g.reshape(1, d), sc3, sh3, w)


def _rwkv_prep_kernel(p_ref, prev_ref, mu_ref, vec_ref, lw_ref, e_ref,
                      r_o, w_o, k_o, v_o, kk_o, kka_o, g_o, bv_o, shift_o, carry):
    j = pl.program_id(1)
    p = p_ref[...]
    tt = p.shape[0]
    first = jnp.where(j == 0, prev_ref[...], carry[...])
    rows = lax.broadcasted_iota(I32, p.shape, 0)
    p_prev = jnp.where(rows == 0, first, pltpu.roll(p, 1, 0))
    carry[...] = p[tt - 1:tt, :]
    shift_o[...] = p[tt - 1:tt, :]
    xm = p + (p_prev - p) * mu_ref[...]
    r, k, v, xl = xm[:, 0:HW], xm[:, HW:2 * HW], xm[:, 2 * HW:3 * HW], xm[:, 3 * HW:]
    w0, a0, k_k = vec_ref[0:1, :], vec_ref[1:2, :], vec_ref[2:3, :]
    k_a, r_k = vec_ref[3:4, :], vec_ref[4:5, :]
    z = -(w0 + _bdot(jnp.tanh(xl), lw_ref[0]))
    softplus = jnp.maximum(z, 0.0) + jnp.log(1.0 + jnp.exp(-jnp.abs(z)))
    decay = jnp.exp(-jnp.exp(-softplus - 0.5))
    a = _sigmoid(a0 + _bdot(xl, lw_ref[1]))
    g = _bdot(_sigmoid(xl), lw_ref[2])
    seg_ones = e_ref[...]
    kk = k * k_k
    kk = kk / jnp.maximum(jnp.sqrt(_hdot(kk * kk, seg_ones)), 1e-12)
    k2 = k * (1.0 + (a - 1.0) * k_a)
    r_o[...] = r
    w_o[...] = decay
    k_o[...] = k2
    v_o[...] = v
    kk_o[...] = kk
    kka_o[...] = kk * a
    g_o[...] = g
    bv_o[...] = _hdot(r * k2 * r_k, seg_ones) * v


def rwkv_prep(p3, prev, mu, vecs, lora_w, seg_ones):
    b, t_len, _ = p3.shape
    tt = min(t_len, 256)
    tok = pl.BlockSpec((None, tt, HW), lambda i, j: (i, j, 0))
    row = pl.BlockSpec((None, 1, RWKV_PAD_W), lambda i, j: (i, 0, 0))
    tok_shape = jax.ShapeDtypeStruct((b, t_len, HW), F32)
    return pl.pallas_call(
        _rwkv_prep_kernel,
        grid=(b, t_len // tt),
        in_specs=[pl.BlockSpec((None, tt, RWKV_PAD_W), lambda i, j: (i, j, 0)), row,
                  _full((1, RWKV_PAD_W)), _full((8, HW)), _full((3, LORA_W, HW)), _full((HW, HW))],
        out_specs=[tok] * 8 + [row],
        out_shape=[tok_shape] * 8 + [jax.ShapeDtypeStruct((b, 1, RWKV_PAD_W), F32)],
        scratch_shapes=[pltpu.VMEM((1, RWKV_PAD_W), F32)],
        compiler_params=_cparams("parallel", "arbitrary"),
        name="rwkv_prep",
    )(p3, prev, mu, vecs, lora_w, seg_ones)


SCAN_BLOCK = 128
SCAN_SUB = 32


def _rwkv_scan_kernel(w_ref, kk_ref, kka_ref, k_ref, r_ref, v_ref, s0_ref, o_ref, sfin_ref, st, xt,
                      *, n_valid):
    c = pl.program_id(1)

    @pl.when(c == 0)
    def _():
        st[...] = s0_ref[...]

    for idx, ref in enumerate((w_ref, kk_ref, kka_ref, k_ref, r_ref)):
        xt[idx] = ref[...].T
    if n_valid < SCAN_BLOCK:
        o_ref[...] = jnp.zeros(o_ref.shape, F32)
    low_half = lax.broadcasted_iota(I32, (HEAD_DIM, LANES), 1) < HEAD_DIM
    n_sub = -(-n_valid // SCAN_SUB)
    n_step = min(SCAN_SUB, n_valid)

    def sub_block(sb, carry):
        shift = ((SCAN_BLOCK // SCAN_SUB - sb) % (SCAN_BLOCK // SCAN_SUB)) * SCAN_SUB
        for pair in range(HW // LANES):
            lanes = slice(pair * LANES, (pair + 1) * LANES)
            tiles = [pltpu.roll(xt[i, lanes, :], shift, 1) for i in range(5)]
            s = st[:, lanes]
            for t8 in range(n_step // 8):
                base = pl.multiple_of(sb * SCAN_SUB + t8 * 8, 8)
                v8 = v_ref[pl.ds(base, 8), lanes]
                out_rows = []
                for u in range(8):
                    t = t8 * 8 + u
                    wt, kkt, kkat, kt, rt = [
                        jnp.where(low_half, x[0:HEAD_DIM, t:t + 1], x[HEAD_DIM:2 * HEAD_DIM, t:t + 1])
                        for x in tiles]
                    sa = -jnp.sum(s * kkt, axis=0, keepdims=True)
                    s = s * wt + kkat * sa + kt * v8[u:u + 1, :]
                    out_rows.append(jnp.sum(s * rt, axis=0, keepdims=True))
                o_ref[pl.ds(base, 8), lanes] = jnp.concatenate(out_rows, axis=0)
            st[:, lanes] = s
        return carry

    lax.fori_loop(0, n_sub, sub_block, 0)
    sfin_ref[...] = st[...]


def rwkv_scan(w, kk, kka, k2, r, v, s0t, n_valid):
    b, tp, _ = w.shape
    tok = pl.BlockSpec((None, SCAN_BLOCK, HW), lambda i, j: (i, j, 0))
    state = pl.BlockSpec((None, HEAD_DIM, HW), lambda i, j: (i, 0, 0))
    return pl.pallas_call(
        functools.partial(_rwkv_scan_kernel, n_valid=n_valid),
        grid=(b, tp // SCAN_BLOCK),
        in_specs=[tok] * 6 + [state],
        out_specs=[tok, state],
        out_shape=[jax.ShapeDtypeStruct((b, tp, HW), F32), jax.ShapeDtypeStruct((b, HEAD_DIM, HW), F32)],
        scratch_shapes=[pltpu.VMEM((HEAD_DIM, HW), F32), pltpu.VMEM((5, HW, SCAN_BLOCK), F32)],
        compiler_params=_cparams("parallel", "arbitrary"),
        name="rwkv_scan",
    )(w, kk, kka, k2, r, v, s0t)


def _dsa_prep_kernel(kiwi_ref, qk_ref, qg_ref, kg_ref, ig_ref, ib_ref, e_ref, qn_o, kn_o, kiw_o):
    seg_mean = e_ref[...]
    q, k = qk_ref[:, 0:HW], qk_ref[:, HW:2 * HW]
    qn_o[...] = q * lax.rsqrt(_hdot(q * q, seg_mean) + RMS_EPS) * qg_ref[...]
    kn_o[...] = k * lax.rsqrt(_hdot(k * k, seg_mean) + RMS_EPS) * kg_ref[...]
    x = kiwi_ref[...]
    is_key = lax.broadcasted_iota(I32, x.shape, 1) < HEAD_DIM
    mu = jnp.sum(jnp.where(is_key, x, 0.0), axis=-1, keepdims=True) / HEAD_DIM
    dev = jnp.where(is_key, x - mu, 0.0)
    var = jnp.sum(dev * dev, axis=-1, keepdims=True) / HEAD_DIM
    kin = dev * lax.rsqrt(var + LN_EPS) * ig_ref[...] + ib_ref[...]
    kiw_o[...] = jnp.where(is_key, kin, x * N_HEADS ** -0.5)


def dsa_prep(p2, qg, kg, ig, ib, seg_mean):
    n = p2.shape[0]
    tm = min(n, 512)
    return pl.pallas_call(
        _dsa_prep_kernel,
        grid=(n // tm,),
        in_specs=[pl.BlockSpec((tm, LANES), lambda i: (i, C_KIWI // LANES)),
                  pl.BlockSpec((tm, 2 * HW), lambda i: (i, C_Q // (2 * HW))),
                  _full((1, HW)), _full((1, HW)), _full((1, LANES)), _full((1, LANES)), _full((HW, HW))],
        out_specs=[pl.BlockSpec((tm, HW), lambda i: (i, 0)), pl.BlockSpec((tm, HW), lambda i: (i, 0)),
                   pl.BlockSpec((tm, LANES), lambda i: (i, 0))],
        out_shape=[jax.ShapeDtypeStruct((n, HW), F32), jax.ShapeDtypeStruct((n, HW), F32),
                   jax.ShapeDtypeStruct((n, LANES), F32)],
        compiler_params=_cparams("parallel"),
        name="dsa_prep",
    )(p2, p2, qg, kg, ig, ib, seg_mean)


INT_MIN = -2 ** 31
SEL_CHUNK = 512


def _order_key(score):
    bits = lax.bitcast_convert_type(score + 0.0, I32)
    return bits ^ ((bits >> 31) & 0x7FFFFFFF)


def _count_ge(key_ref, n_chunks, cand, rows, strict=False):
    def body(c, acc):
        blk = key_ref[:, pl.ds(pl.multiple_of(c * SEL_CHUNK, SEL_CHUNK), SEL_CHUNK)]
        hit = (blk > cand) if strict else (blk >= cand)
        ones = jnp.where(hit, 1.0, 0.0)
        for g in range(SEL_CHUNK // LANES):
            acc = acc + ones[:, g * LANES:(g + 1) * LANES]
        return acc
    acc = lax.fori_loop(0, n_chunks, body, jnp.zeros((rows, LANES), F32))
    return jnp.sum(acc, axis=1, keepdims=True)


def _kth_largest_key(key_ref, n_chunks, n_sel, rows):
    cnt0 = _count_ge(key_ref, n_chunks, jnp.zeros((rows, 1), I32), rows)
    thr = jnp.where(cnt0 >= n_sel, 0, INT_MIN).astype(I32)

    def bit_body(it, thr):
        cand = thr + jnp.left_shift(jnp.int32(1), 30 - it)
        cnt = _count_ge(key_ref, n_chunks, cand, rows)
        return jnp.where(cnt >= n_sel, cand, thr)

    return lax.fori_loop(0, 31, bit_body, thr)


def _select_into(key_ref, valid_fn, n_chunks, n_sel, rows, tri_ref, write_fn):
    thr = _kth_largest_key(key_ref, n_chunks, n_sel, rows)
    n_gt = _count_ge(key_ref, n_chunks, thr, rows, strict=True)
    need = n_sel - n_gt

    def chunk(c):
        return key_ref[:, pl.ds(pl.multiple_of(c * SEL_CHUNK, SEL_CHUNK), SEL_CHUNK)]

    def count_ties(c, acc):
        return acc + jnp.sum(jnp.where((chunk(c) == thr) & valid_fn(c), 1.0, 0.0), axis=1, keepdims=True)

    n_eq = lax.fori_loop(0, n_chunks, count_ties, jnp.zeros((rows, 1), F32))
    surplus = jnp.max(n_eq - need) > 0.0

    @pl.when(jnp.logical_not(surplus))
    def _():
        def body(c, carry):
            blk = chunk(c)
            write_fn(c, (blk >= thr) & valid_fn(c))
            return carry
        lax.fori_loop(0, n_chunks, body, 0)

    @pl.when(surplus)
    def _():
        def body(c, seen):
            blk = chunk(c)
            val = valid_fn(c)
            tie = jnp.where((blk == thr) & val, 1.0, 0.0)
            parts = []
            for g in range(SEL_CHUNK // LANES):
                tg = tie[:, g * LANES:(g + 1) * LANES]
                before = _bdot(tg, tri_ref[...]) + seen
                parts.append(jnp.where(before < need, tg, 0.0))
                seen = seen + jnp.sum(tg, axis=1, keepdims=True)
            take = jnp.concatenate(parts, axis=1) > 0.0
            write_fn(c, ((blk > thr) & val) | take)
            return seen
        lax.fori_loop(0, n_chunks, body, jnp.zeros((rows, 1), F32))


def _dsa_select_kernel(qi_ref, ki_ref, wi_ref, tri_ref, m_ref, key_scr, *, n_sel, qb):
    i = pl.program_id(1)
    n_chunks = (i * qb + qb + SEL_CHUNK - 1) // SEL_CHUNK
    m_ref[...] = jnp.zeros(m_ref.shape, m_ref.dtype)
    qpos = i * qb + lax.broadcasted_iota(I32, (qb, SEL_CHUNK), 0)
    lane_pos = lax.broadcasted_iota(I32, (qb, SEL_CHUNK), 1)
    wi = wi_ref[...]

    def causal(c):
        return lane_pos + c * SEL_CHUNK <= qpos

    def score_chunk(c, carry):
        off = pl.multiple_of(c * SEL_CHUNK, SEL_CHUNK)
        s = jnp.maximum(_bdot_nt(qi_ref[...], ki_ref[pl.ds(off, SEL_CHUNK), :]), 0.0)
        acc = jnp.zeros((qb, SEL_CHUNK), F32)
        for h in range(N_HEADS):
            acc = acc + s[h * qb:(h + 1) * qb, :] * wi[:, HEAD_DIM + h:HEAD_DIM + h + 1]
        key_scr[:, pl.ds(off, SEL_CHUNK)] = _order_key(jnp.where(causal(c), acc, NEG_INF))
        return carry

    lax.fori_loop(0, n_chunks, score_chunk, 0)

    def write(c, sel):
        off = pl.multiple_of(c * SEL_CHUNK, SEL_CHUNK)
        m_ref[:, pl.ds(off, SEL_CHUNK)] = jnp.where(sel, 1, 0).astype(m_ref.dtype)

    _select_into(key_scr, causal, n_chunks, n_sel, qb, tri_ref, write)


def dsa_select(qi_stack, ki, kiw3, tri, n_sel):
    b, nb, rows, _ = qi_stack.shape
    qb = rows // N_HEADS
    s_len = ki.shape[1]
    return pl.pallas_call(
        functools.partial(_dsa_select_kernel, n_sel=n_sel, qb=qb),
        grid=(b, nb),
        in_specs=[pl.BlockSpec((None, None, rows, HEAD_DIM), lambda bi, i: (bi, i, 0, 0)),
                  pl.BlockSpec((None, s_len, HEAD_DIM), lambda bi, i: (bi, 0, 0)),
                  pl.BlockSpec((None, qb, LANES), lambda bi, i: (bi, i, 0)),
                  _full((LANES, LANES))],
        out_specs=pl.BlockSpec((None, qb, s_len), lambda bi, i: (bi, i, 0)),
        out_shape=jax.ShapeDtypeStruct((b, s_len, s_len), jnp.int8),
        scratch_shapes=[pltpu.VMEM((qb, s_len), I32)],
        compiler_params=_cparams("parallel", "arbitrary"),
        name="dsa_select",
    )(qi_stack, ki, kiw3, tri)


ATT_CHUNK = 512
ATT_NEAR = 256


def _softmax_step(s, mk, m, l, acc, v):
    s = jnp.where(mk, s, NEG_INF)
    m_new = jnp.maximum(m, jnp.max(s, axis=1, keepdims=True))
    alpha = jnp.exp(m - m_new)
    p = jnp.where(mk, jnp.exp(s - m_new), 0.0)
    l = l * alpha + jnp.sum(p, axis=1, keepdims=True)
    acc = acc * alpha + _bdot(p, v)
    return m_new, l, acc


def _dsa_attn_kernel(q_ref, k_ref, v_ref, m_ref, toe_ref, far_ref, o_ref, *, qb):
    i = pl.program_id(2)
    near_start = jnp.maximum(i - 1, 0) * qb
    n_far = (near_start + ATT_CHUNK - 1) // ATT_CHUNK
    lane = lax.broadcasted_iota(I32, (qb, LANES), 1)
    kpos = lax.broadcasted_iota(I32, (qb, ATT_CHUNK), 1)
    q = q_ref[...]
    outs = []
    for e in range(2):
        qh = jnp.where((lane < HEAD_DIM) == (e == 0), q, jnp.zeros_like(q))
        far_bias = far_ref[e:e + 1, :][:, 0:1]

        def far_chunk(c, carry):
            m, l, acc = carry
            off = pl.multiple_of(c * ATT_CHUNK, ATT_CHUNK)
            s = _bdot_nt(qh, k_ref[pl.ds(off, ATT_CHUNK), :]) + far_bias
            mk = (m_ref[:, pl.ds(off, ATT_CHUNK)].astype(I32) != 0) & (kpos + off < near_start)
            return _softmax_step(s, mk, m, l, acc, v_ref[pl.ds(off, ATT_CHUNK), :])

        init = (jnp.full((qb, 1), NEG_INF, F32), jnp.zeros((qb, 1), F32), jnp.zeros((qb, LANES), F32))
        m, l, acc = lax.fori_loop(0, n_far, far_chunk, init)
        off = pl.multiple_of(near_start, qb)
        s = _bdot_nt(qh, k_ref[pl.ds(off, ATT_NEAR), :]) + toe_ref[e]
        mk = m_ref[:, pl.ds(off, ATT_NEAR)].astype(I32) != 0
        m, l, acc = _softmax_step(s, mk, m, l, acc, v_ref[pl.ds(off, ATT_NEAR), :])
        outs.append(acc / l)
    o_ref[...] = jnp.where(lane < HEAD_DIM, outs[0], outs[1])


def dsa_attn(q_pairs, k_pairs, v_pairs, mask, toe, far):
    b, n_pair, s_len, _ = q_pairs.shape
    qb = 128
    kv = pl.BlockSpec((None, None, s_len, LANES), lambda bi, p, i: (bi, p, 0, 0))
    return pl.pallas_call(
        functools.partial(_dsa_attn_kernel, qb=qb),
        grid=(b, n_pair, s_len // qb),
        in_specs=[pl.BlockSpec((None, None, qb, LANES), lambda bi, p, i: (bi, p, i, 0)), kv, kv,
                  pl.BlockSpec((None, qb, s_len), lambda bi, p, i: (bi, i, 0)),
                  pl.BlockSpec((None, None, 2, qb, ATT_NEAR), lambda bi, p, i: (jnp.minimum(i, 1), p, 0, 0, 0)),
                  pl.BlockSpec((None, 8, LANES), lambda bi, p, i: (p, 0, 0))],
        out_specs=pl.BlockSpec((None, qb, LANES), lambda bi, p, i: (bi, i, p)),
        out_shape=jax.ShapeDtypeStruct((b, s_len, n_pair * LANES), F32),
        compiler_params=_cparams("parallel", "parallel", "arbitrary"),
        name="dsa_attn",
    )(q_pairs, k_pairs, v_pairs, mask, toe, far)


def _dsa_sample_kernel(pt_ref, qi_ref, wcol_ref, q_ref, cki_ref, nki_ref, ck_ref, nk_ref, cv_ref, nv_ref,
                       bias_ref, tri_ref, o_ref, key_scr, sel_scr, m_scr, l_scr, acc_scr,
                       *, n_pages, n_sel, t_new, page):
    phase = pl.program_id(1)
    pg = pl.program_id(2)
    rows = N_HEADS * t_new
    n_chunks = -(-(n_pages + 1) * page // SEL_CHUNK)
    lane_t = lax.broadcasted_iota(I32, (t_new, page), 1)
    row_t = lax.broadcasted_iota(I32, (t_new, page), 0)
    is_new = pg == n_pages

    @pl.when(jnp.logical_and(phase == 0, pg == 0))
    def _():
        key_scr[...] = jnp.full(key_scr.shape, INT_MIN, I32)

    @pl.when(phase == 0)
    def _():
        ki = jnp.where(is_new, nki_ref[...], cki_ref[...])
        s = jnp.maximum(_bdot_nt(qi_ref[...], ki), 0.0) * wcol_ref[...]
        acc = jnp.zeros((t_new, page), F32)
        for h in range(N_HEADS):
            acc = acc + s[h * t_new:(h + 1) * t_new, :]
        ok = jnp.logical_or(jnp.logical_not(is_new), lane_t <= row_t)
        key_scr[:, pl.ds(pl.multiple_of(pg * page, page), page)] = _order_key(jnp.where(ok, acc, NEG_INF))

    @pl.when(jnp.logical_and(phase == 1, pg == 0))
    def _():
        pos = lax.broadcasted_iota(I32, (t_new, SEL_CHUNK), 1)
        qpos = n_pages * page + lax.broadcasted_iota(I32, (t_new, SEL_CHUNK), 0)

        def valid(c):
            return pos + c * SEL_CHUNK <= qpos

        def write(c, sel):
            sel_scr[:, pl.ds(pl.multiple_of(c * SEL_CHUNK, SEL_CHUNK), SEL_CHUNK)] = jnp.where(sel, 1.0, 0.0)

        _select_into(key_scr, valid, n_chunks, n_sel, t_new, tri_ref, write)
        m_scr[...] = jnp.full(m_scr.shape, NEG_INF, F32)
        l_scr[...] = jnp.zeros(l_scr.shape, F32)
        acc_scr[...] = jnp.zeros(acc_scr.shape, F32)

    @pl.when(phase == 1)
    def _():
        k = jnp.where(is_new, nk_ref[...], ck_ref[...])
        v = jnp.where(is_new, nv_ref[...], cv_ref[...])
        head_of_row = lax.broadcasted_iota(I32, (rows, HW), 0) // t_new
        head_of_lane = lax.broadcasted_iota(I32, (rows, HW), 1) // HEAD_DIM
        q_all = jnp.concatenate([q_ref[...]] * N_HEADS, axis=0)
        qx = jnp.where(head_of_row == head_of_lane, q_all, 0.0)
        s = _bdot_nt(qx, k) + bias_ref[...]
        sel = sel_scr[:, pl.ds(pl.multiple_of(pg * page, page), page)]
        mk = jnp.concatenate([sel] * N_HEADS, axis=0) > 0.0
        m, l, acc = _softmax_step(s, mk, m_scr[...], l_scr[...], acc_scr[...], v)
        m_scr[...] = m
        l_scr[...] = l
        acc_scr[...] = acc

        @pl.when(is_new)
        def _():
            o_full = jnp.where(head_of_row == head_of_lane, acc / l, 0.0)
            out = jnp.zeros((t_new, HW), F32)
            for h in range(N_HEADS):
                out = out + o_full[h * t_new:(h + 1) * t_new, :]
            o_ref[...] = out


def dsa_sample(page_table, qi_rows, wcol, q3, cache_ki, new_ki, cache_k, new_k, cache_v, new_v, bias_tab, tri,
               n_sel):
    db, n_pages = page_table.shape
    page = cache_ki.shape[1]
    t_new = q3.shape[1]
    rows = N_HEADS * t_new
    n_chunks = -(-(n_pages + 1) * page // SEL_CHUNK)
    last = n_pages - 1

    def cache_map(active_phase):
        def index(b, ph, pg, pt):
            use = jnp.where(ph == active_phase, jnp.minimum(pg, last), 0)
            return (pt[b * n_pages + use], 0, 0)
        return index

    def per_b(shape):
        return pl.BlockSpec((None,) + shape, lambda b, ph, pg, pt: (b, 0, 0))

    def bias_index(b, ph, pg, pt):
        return (jnp.where(pg == n_pages, 2, jnp.where(pg == last, 1, 0)), 0, 0)

    grid_spec = pltpu.PrefetchScalarGridSpec(
        num_scalar_prefetch=1,
        grid=(db, 2, n_pages + 1),
        in_specs=[per_b((rows, HEAD_DIM)), per_b((rows, LANES)), per_b((t_new, HW)),
                  pl.BlockSpec((None, page, HEAD_DIM), cache_map(0)), per_b((page, HEAD_DIM)),
                  pl.BlockSpec((None, page, HW), cache_map(1)), per_b((page, HW)),
                  pl.BlockSpec((None, page, HW), cache_map(1)), per_b((page, HW)),
                  pl.BlockSpec((None, rows, page), bias_index),
                  pl.BlockSpec((LANES, LANES), lambda b, ph, pg, pt: (0, 0))],
        out_specs=pl.BlockSpec((None, t_new, HW), lambda b, ph, pg, pt: (b, 0, 0)),
        scratch_shapes=[pltpu.VMEM((t_new, n_chunks * SEL_CHUNK), I32),
                        pltpu.VMEM((t_new, n_chunks * SEL_CHUNK), F32),
                        pltpu.VMEM((rows, 1), F32), pltpu.VMEM((rows, 1), F32), pltpu.VMEM((rows, HW), F32)])
    return pl.pallas_call(
        functools.partial(_dsa_sample_kernel, n_pages=n_pages, n_sel=n_sel, t_new=t_new, page=page),
        grid_spec=grid_spec,
        out_shape=jax.ShapeDtypeStruct((db, t_new, HW), F32),
        compiler_params=_cparams("parallel", "arbitrary", "arbitrary"),
        name="dsa_sample",
    )(page_table.reshape(-1), qi_rows, wcol, q3, cache_ki, new_ki, cache_k, new_k, cache_v, new_v, bias_tab, tri)


def _retention_kernel(q_ref, k_ref, v_ref, g_ref, cos_ref, sin_ref, dmask_ref, qdec_ref, kdec_ref, cdec_ref,
                      gn_ref, s0_ref, o_ref, sfin_ref, st, *, rows):
    c = pl.program_id(1)

    @pl.when(c == 0)
    def _():
        st[...] = s0_ref[...]

    def padded(x):
        if x.shape[0] == RET_CHUNK:
            return x
        return jnp.concatenate([x, jnp.zeros((RET_CHUNK - x.shape[0], x.shape[1]), x.dtype)], axis=0)

    lane = lax.broadcasted_iota(I32, (RET_CHUNK, HW), 1)
    first_half = (lane % HEAD_DIM) < HEAD_DIM // 2

    def rope(x):
        partner = jnp.where(first_half, pltpu.roll(x, HW - HEAD_DIM // 2, 1), pltpu.roll(x, HEAD_DIM // 2, 1))
        return x * padded(cos_ref[...]) + partner * padded(sin_ref[...])

    q = rope(padded(q_ref[...]))
    k = rope(padded(k_ref[...])) * HEAD_DIM ** -0.5
    v = padded(v_ref[...])
    state = st[...]
    head_of_lane = lane // HEAD_DIM
    qdec = qdec_ref[...]
    o_parts = []
    for h in range(N_HEADS):
        qh = jnp.where(head_of_lane == h, q, 0.0)
        inner = _bdot_nt(qh, k) * dmask_ref[h]
        vh = v[:, h * RET_DV:(h + 1) * RET_DV]
        oh = _bdot(inner, vh) + _bdot(qh, state) * qdec[:, h:h + 1]
        mu = jnp.mean(oh, axis=-1, keepdims=True)
        dev = oh - mu
        var = jnp.mean(dev * dev, axis=-1, keepdims=True)
        o_parts.append(dev * lax.rsqrt(var + LN_EPS))
    o = jnp.concatenate(o_parts, axis=1) * gn_ref[...]
    gate = g_ref[...]
    o_ref[...] = gate * _sigmoid(gate) * o[0:rows, :]
    kv = _bdot((k * kdec_ref[...]).T, v)
    upd = jnp.concatenate([kv[h * HEAD_DIM:(h + 1) * HEAD_DIM, h * RET_DV:(h + 1) * RET_DV]
                           for h in range(N_HEADS)], axis=0)
    st[...] = state * cdec_ref[...] + upd
    sfin_ref[...] = st[...]


def retention(p3, pos0, s0, gn_g):
    b, t_len, _ = p3.shape
    chunk = math.gcd(t_len, RET_CHUNK)
    n_chunk = t_len // chunk
    half = HEAD_DIM // 2
    inv = ROPE_BASE ** (-jnp.arange(half, dtype=F32) / half)
    ang = (pos0 + jnp.arange(t_len)).astype(F32)[:, None] * inv[None, :]
    cos = jnp.tile(jnp.cos(ang), (1, 2 * N_HEADS))
    sin = jnp.tile(jnp.concatenate([-jnp.sin(ang), jnp.sin(ang)], axis=1), (1, N_HEADS))
    log_g = jnp.log(1.0 - 2.0 ** (-5.0 - jnp.arange(N_HEADS, dtype=F32)))
    idx = jnp.arange(RET_CHUNK, dtype=F32)
    rel = idx[:, None] - idx[None, :]
    dmask = jnp.where(rel[None] >= 0, jnp.exp(rel[None] * log_g[:, None, None]), 0.0)
    qdec = jnp.pad(jnp.exp((idx + 1.0)[:, None] * log_g[None, :]), ((0, 0), (0, LANES - N_HEADS)))
    kdec = jnp.where((idx < chunk)[:, None], jnp.exp((chunk - 1.0 - idx)[:, None] * log_g[None, :]), 0.0)
    kdec = jnp.repeat(kdec, HEAD_DIM, axis=1)
    cdec = jnp.broadcast_to(jnp.repeat(jnp.exp(chunk * log_g), HEAD_DIM)[:, None], (HW, RET_DV))
    tok = lambda width, col: pl.BlockSpec((None, chunk, width), lambda i, j: (i, j, col))
    state = pl.BlockSpec((None, HW, RET_DV), lambda i, j: (i, 0, 0))
    return pl.pallas_call(
        functools.partial(_retention_kernel, rows=chunk),
        grid=(b, n_chunk),
        in_specs=[tok(HW, C_RQ // HW), tok(HW, C_RK // HW), tok(RET_V_W, C_RV // RET_V_W),
                  tok(RET_V_W, C_RG // RET_V_W),
                  pl.BlockSpec((chunk, HW), lambda i, j: (j, 0)), pl.BlockSpec((chunk, HW), lambda i, j: (j, 0)),
                  _full((N_HEADS, RET_CHUNK, RET_CHUNK)), _full((RET_CHUNK, LANES)), _full((RET_CHUNK, HW)),
                  _full((HW, RET_DV)), _full((1, RET_V_W)), state],
        out_specs=[pl.BlockSpec((None, chunk, RET_V_W), lambda i, j: (i, j, 0)), state],
        out_shape=[jax.ShapeDtypeStruct((b, t_len, RET_V_W), F32), jax.ShapeDtypeStruct((b, HW, RET_DV), F32)],
        scratch_shapes=[pltpu.VMEM((HW, RET_DV), F32)],
        compiler_params=_cparams("parallel", "arbitrary"),
        name="retention",
    )(p3, p3, p3, p3, cos, sin, dmask, qdec, kdec, cdec, gn_g.reshape(1, RET_V_W), s0)


def _merge_kernel(x_ref, ra_ref, bv_ref, g_ref, ob_ref, oc_ref, ga_ref, gb_ref, gc_ref,
                  gt1_ref, sc2_ref, sh2_ref, lng_ref, lnb_ref, e_ref, wa_ref, wb_ref, wc_ref, wo_ref,
                  n2_ref, rw_ref, rb_ref, x1_o, h2_o, idx_o, gate_o):
    seg_mean = e_ref[...]
    ra = ra_ref[...]
    mu = _hdot(ra, seg_mean)
    dev = ra - mu
    var = _hdot(dev * dev, seg_mean)
    oa = (dev * lax.rsqrt(var + RWKV_GN_EPS) * lng_ref[...] + lnb_ref[...] + bv_ref[...]) * g_ref[...]
    merged = (_sigmoid(ga_ref[...]) * _bdot(oa, wa_ref[...])
              + _sigmoid(gb_ref[...]) * _bdot(ob_ref[...], wb_ref[...])
              + _sigmoid(gc_ref[...]) * _bdot(oc_ref[...], wc_ref[...]))
    x1 = x_ref[...] + gt1_ref[...] * _bdot(merged, wo_ref[...])
    x1_o[...] = x1
    y = x1 * lax.rsqrt(jnp.mean(x1 * x1, axis=-1, keepdims=True) + RMS_EPS) * n2_ref[...]
    h2 = y * (1.0 + sc2_ref[...]) + sh2_ref[...]
    h2_o[...] = h2
    logits = _hdot(h2, rw_ref[...]) + rb_ref[...]
    lane = lax.broadcasted_iota(I32, logits.shape, 1).astype(F32)
    idx_acc = jnp.zeros(logits.shape, F32)
    val_acc = jnp.zeros(logits.shape, F32)
    top = None
    denom = jnp.zeros((logits.shape[0], 1), F32)
    for k in range(TOP_K):
        m = jnp.max(logits, axis=-1, keepdims=True)
        ix = jnp.min(jnp.where(logits == m, lane, float(LANES)), axis=-1, keepdims=True)
        top = m if top is None else top
        ev = jnp.exp(m - top)
        denom = denom + ev
        idx_acc = jnp.where(lane == k, ix, idx_acc)
        val_acc = jnp.where(lane == k, ev, val_acc)
        logits = jnp.where(lane == ix, -jnp.inf, logits)
    idx_o[...] = idx_acc.astype(I32)
    gate_o[...] = val_acc / denom


def merge(x2, t_len, p2, ra, bv, g, ob, oc, gt1, sc2, sh2, lw):
    n, d = x2.shape
    tm = min(n, 256)
    mods = [_mod_rows(m, t_len, tm) for m in (gt1, sc2, sh2)]
    row = lambda width, col=0: pl.BlockSpec((tm, width), lambda i: (i, col))
    gate_col = C_GATE // d
    out_shape = [jax.ShapeDtypeStruct((n, d), F32), jax.ShapeDtypeStruct((n, d), F32),
                 jax.ShapeDtypeStruct((n, LANES), I32), jax.ShapeDtypeStruct((n, LANES), F32)]
    return pl.pallas_call(
        _merge_kernel,
        grid=(n // tm,),
        in_specs=[row(d), row(HW), row(HW), row(HW), row(HW), row(RET_V_W),
                  row(d, gate_col), row(d, gate_col + 1), row(d, gate_col + 2)]
                 + [m[1] for m in mods]
                 + [_full((1, HW)), _full((1, HW)), _full((HW, HW)), _full((HW, d)), _full((HW, d)),
                    _full((RET_V_W, d)), _full((d, d)), _full((1, d)), _full((d, LANES)), _full((1, LANES))],
        out_specs=[row(d), row(d), row(LANES), row(LANES)],
        out_shape=out_shape,
        compiler_params=_cparams("parallel"),
        name="merge",
    )(x2, ra, bv, g, ob, oc, p2, p2, p2, *[m[0] for m in mods],
      lw['ln_g'], lw['ln_b'], lw['seg_mean'], lw['w_up_a'], lw['w_up_b'], lw['w_up_c'], lw['w_out'],
      lw['norm2_g'], lw['router_w'], lw['router_b'])


def _moe_ffn_kernel(be_ref, used_ref, x_ref, w1g_ref, w1l_ref, b1g_ref, b1l_ref, w2_ref, b2_ref, o_ref):
    i = pl.program_id(0)
    rows = x_ref.shape[0]

    @pl.when(i * rows < used_ref[0])
    def _():
        x = x_ref[...].astype(BF16)
        glu = jnp.minimum(jnp.dot(x, w1g_ref[...], preferred_element_type=F32) + b1g_ref[...], SWIGLU_LIMIT)
        lin = jnp.clip(jnp.dot(x, w1l_ref[...], preferred_element_type=F32) + b1l_ref[...],
                       -SWIGLU_LIMIT, SWIGLU_LIMIT)
        act = glu * _sigmoid(SWIGLU_ALPHA * glu) * (lin + 1.0)
        o_ref[...] = _bdot(act, w2_ref[...]) + b2_ref[...]

    @pl.when(i * rows >= used_ref[0])
    def _():
        o_ref[...] = jnp.zeros(o_ref.shape, F32)


def moe_ffn(block_expert, n_used, x_rows, lw, rows):
    n_rows, d = x_rows.shape
    n_blocks = n_rows // rows
    d_ff = lw['moe_w2'].shape[1]
    wspec = lambda a, b_: pl.BlockSpec((None, a, b_), lambda i, be, nu: (be[i], 0, 0))
    grid_spec = pltpu.PrefetchScalarGridSpec(
        num_scalar_prefetch=2,
        grid=(n_blocks,),
        in_specs=[pl.BlockSpec((rows, d), lambda i, be, nu: (i, 0)),
                  wspec(d, d_ff), wspec(d, d_ff), wspec(1, d_ff), wspec(1, d_ff), wspec(d_ff, d), wspec(1, d)],
        out_specs=pl.BlockSpec((rows, d), lambda i, be, nu: (i, 0)))
    return pl.pallas_call(
        _moe_ffn_kernel,
        grid_spec=grid_spec,
        out_shape=jax.ShapeDtypeStruct((n_rows, d), F32),
        compiler_params=_cparams("arbitrary"),
        name="moe_ffn",
    )(block_expert, n_used, x_rows, lw['moe_w1g'], lw['moe_w1l'], lw['moe_b1g'], lw['moe_b1l'],
      lw['moe_w2'], lw['moe_b2'])


def _combine_kernel(x_ref, y_ref, gate_ref, gt2_ref, o_ref):
    gate = gate_ref[...]
    y = jnp.zeros(x_ref.shape, F32)
    for k in range(TOP_K):
        y = y + gate[:, k:k + 1] * y_ref[k]
    o_ref[...] = x_ref[...] + gt2_ref[...] * y


def combine(x1, t_len, y4, gate, gt2):
    n, d = x1.shape
    tm = min(n, 256)
    gt2_3, gt2_spec = _mod_rows(gt2, t_len, tm)
    return pl.pallas_call(
        _combine_kernel,
        grid=(n // tm,),
        in_specs=[pl.BlockSpec((tm, d), lambda i: (i, 0)), pl.BlockSpec((TOP_K, tm, d), lambda i: (0, i, 0)),
                  pl.BlockSpec((tm, LANES), lambda i: (i, 0)), gt2_spec],
        out_specs=pl.BlockSpec((tm, d), lambda i: (i, 0)),
        out_shape=jax.ShapeDtypeStruct((n, d), F32),
        compiler_params=_cparams("parallel"),
        name="combine",
    )(x1, y4, gate, gt2_3)


def moe(x1, t_len, h2, top_idx, gate, gt2, lw):
    n, d = h2.shape
    rows = 256 if n * TOP_K >= 8192 else 64
    a_n = n * TOP_K
    e_flat = top_idx[:, :TOP_K].reshape(-1)
    onehot = (e_flat[:, None] == jnp.arange(N_EXPERTS, dtype=I32)[None, :]).astype(I32)
    ranks = jnp.cumsum(onehot, axis=0) - onehot
    rank = jnp.sum(ranks * onehot, axis=1)
    counts = jnp.sum(onehot, axis=0)
    padded = (counts + rows - 1) // rows * rows
    pends = jnp.cumsum(padded)
    pstarts = pends - padded
    dest = (pstarts[e_flat] + rank).astype(I32)
    n_blocks = -(-a_n // rows) + N_EXPERTS
    tok_flat = jnp.repeat(jnp.arange(n, dtype=I32), TOP_K)
    row_tok = jnp.zeros(n_blocks * rows, I32).at[dest].set(tok_flat)
    block_expert = jnp.minimum(jnp.searchsorted(pends, jnp.arange(n_blocks, dtype=I32) * rows, side='right'),
                               N_EXPERTS - 1).astype(I32)
    x_rows = h2[row_tok]
    y_rows = moe_ffn(block_expert, pends[-1:].astype(I32), x_rows, lw, rows)
    y4 = y_rows[dest.reshape(n, TOP_K).T]
    return combine(x1, t_len, y4, gate, gt2)


def _t5_bucket(dist):
    max_exact = REL_BUCKETS // 2
    d = jnp.maximum(dist, 0)
    large = max_exact + (jnp.log(jnp.maximum(d, 1).astype(F32) / max_exact)
                         / math.log(REL_MAX_DIST / max_exact) * (REL_BUCKETS - max_exact)).astype(I32)
    large = jnp.minimum(large, REL_BUCKETS - 1)
    return jnp.where(d < max_exact, d, large)


def _seg_matrix(value):
    head = np.arange(HW) // HEAD_DIM
    return jnp.asarray((head[:, None] == head[None, :]).astype(np.float32) * value)


def _pack_layer_weights(l, w):
    d = D_MODEL
    parts = jnp.split(w['w_in'][l], np.cumsum(np.array(IN_WIDTHS))[:-1].tolist(), axis=1)
    p_rwkv, pq, pk, pv, pqi, pki, pwi, rq, rk, rv, rg, pgate = parts
    zeros = lambda n: jnp.zeros((d, n), F32)
    w_in = jnp.concatenate([p_rwkv, zeros(RWKV_PAD_W - RWKV_PROJ_W), pki, pwi, zeros(LANES - HEAD_DIM - N_HEADS),
                            pq, pk, pv, pqi, rq, rk, rv, rg, pgate], axis=1).astype(BF16)
    lora = jnp.zeros((3, LORA_W, HW), F32)
    lora = lora.at[0, 0:DECAY_LORA].set(w['rwkv_w2'][l])
    lora = lora.at[1, DECAY_LORA:DECAY_LORA + AAA_LORA].set(w['rwkv_a2'][l])
    lora = lora.at[2, DECAY_LORA + AAA_LORA:DECAY_LORA + AAA_LORA + GATE_LORA].set(w['rwkv_g2'][l])
    vecs = jnp.zeros((8, HW), F32)
    for i, name in enumerate(('rwkv_w0', 'rwkv_a0', 'rwkv_k_k', 'rwkv_k_a', 'rwkv_r_k')):
        vecs = vecs.at[i].set(w[name][l])
    pad_lanes = lambda v, fill=0.0: jnp.concatenate(
        [v, jnp.full((LANES - v.shape[0],), fill, F32)]).reshape(1, LANES)
    w1 = w['moe_w1'][l]
    b1 = w['moe_b1'][l]
    return dict(
        w_in=w_in, norm1_g=w['norm1_g'][l],
        mu=jnp.pad(w['rwkv_mu'][l], (0, RWKV_PAD_W - RWKV_PROJ_W)).reshape(1, RWKV_PAD_W),
        lora=lora.astype(BF16), vecs=vecs,
        qg=jnp.tile(w['q_norm_g'][l], N_HEADS).reshape(1, HW), kg=jnp.tile(w['k_norm_g'][l], N_HEADS).reshape(1, HW),
        ig=pad_lanes(w['idx_k_norm_g'][l]), ib=pad_lanes(w['idx_k_norm_b'][l]),
        ret_gn_g=w['ret_gn_g'][l],
        ln_g=w['rwkv_ln_g'][l].reshape(1, HW), ln_b=w['rwkv_ln_b'][l].reshape(1, HW),
        w_up_a=w['w_up_rwkv'][l].astype(BF16), w_up_b=w['w_up_att'][l].astype(BF16),
        w_up_c=w['w_up_ret'][l].astype(BF16), w_out=w['w_out'][l].astype(BF16),
        norm2_g=w['norm2_g'][l].reshape(1, d),
        router_w=jnp.pad(w['router_w'][l], ((0, 0), (0, LANES - N_EXPERTS))),
        router_b=pad_lanes(w['router_b'][l], NEG_INF),
        moe_w1g=w1[:, :, 0::2].astype(BF16), moe_w1l=w1[:, :, 1::2].astype(BF16),
        moe_b1g=b1[:, None, 0::2], moe_b1l=b1[:, None, 1::2],
        moe_w2=w['moe_w2'][l].astype(BF16), moe_b2=w['moe_b2'][l][:, None, :],
    )


def _attend_prompt(p3, qn, kn, kiw, rel_bias, tri):
    b, s_len, _ = p3.shape
    qb = 128
    nb = s_len // qb
    n_sel = max(1, min(TOPK_KEYS, s_len // 4))
    scale = HEAD_DIM ** -0.5
    qi = p3[:, :, C_QI:C_QI + HW].reshape(b, nb, qb, N_HEADS, HEAD_DIM).transpose(0, 1, 3, 2, 4)
    qi_stack = (qi * scale).reshape(b, nb, N_HEADS * qb, HEAD_DIM).astype(BF16)
    kiw3 = kiw.reshape(b, s_len, LANES)
    mask = dsa_select(qi_stack, kiw3[:, :, :HEAD_DIM].astype(BF16), kiw3, tri, n_sel)
    pairs = lambda a: a.reshape(b, s_len, HW // LANES, LANES).transpose(0, 2, 1, 3).astype(BF16)
    r = jnp.arange(qb)[:, None]
    c = jnp.arange(ATT_NEAR)[None, :]
    dist = jnp.stack([r - c, qb + r - c])
    toe = rel_bias[_t5_bucket(dist)].transpose(0, 3, 1, 2).reshape(2, HW // LANES, 2, qb, ATT_NEAR)
    far = jnp.zeros((HW // LANES, 8, LANES), F32).at[:, 0:2, :].set(
        jnp.broadcast_to(rel_bias[REL_BUCKETS - 1].reshape(HW // LANES, 2, 1), (HW // LANES, 2, LANES)))
    return dsa_attn(pairs(qn.reshape(b, s_len, HW) * scale), pairs(kn.reshape(b, s_len, HW)),
                    pairs(p3[:, :, C_V:C_V + HW]), mask, toe, far)


def _attend_sample(p3, qn, kn, kiw, rel_bias, tri, cache_k_l, cache_v_l, cache_ki_l, page_table):
    db, t_new, _ = p3.shape
    n_pool, page = cache_ki_l.shape[:2]
    n_pages = page_table.shape[1]
    assert page >= REL_MAX_DIST and t_new <= page
    n_sel = max(1, min(TOPK_KEYS, (n_pages * page + t_new) // 4))
    scale = HEAD_DIM ** -0.5
    rows = N_HEADS * t_new
    qi = p3[:, :, C_QI:C_QI + HW].reshape(db, t_new, N_HEADS, HEAD_DIM).transpose(0, 2, 1, 3)
    qi_rows = (qi * scale).reshape(db, rows, HEAD_DIM).astype(BF16)
    kiw3 = kiw.reshape(db, t_new, LANES)
    wcol = kiw3[:, :, HEAD_DIM:HEAD_DIM + N_HEADS].transpose(0, 2, 1).reshape(db, rows, 1)
    wcol = jnp.broadcast_to(wcol, (db, rows, LANES))
    pad_rows = lambda a: jnp.pad(a, ((0, 0), (0, page - t_new), (0, 0)))
    t_of_row = jnp.tile(jnp.arange(t_new), N_HEADS)[:, None]
    j = jnp.arange(page)[None, :]
    dist = jnp.stack([jnp.full((rows, page), REL_MAX_DIST), page + t_of_row - j, t_of_row - j])
    h_of_row = jnp.repeat(jnp.arange(N_HEADS), t_new)
    bias_tab = rel_bias[_t5_bucket(dist), h_of_row[None, :, None]]
    return dsa_sample(page_table, qi_rows, wcol, qn.reshape(db, t_new, HW) * scale,
                      cache_ki_l, pad_rows(kiw3[:, :, :HEAD_DIM]),
                      cache_k_l.reshape(n_pool, page, HW), pad_rows(kn.reshape(db, t_new, HW)),
                      cache_v_l.reshape(n_pool, page, HW), pad_rows(p3[:, :, C_V:C_V + HW]),
                      bias_tab, tri, n_sel)


def _layer_group(x, mods, lw, consts, rel_bias, pos0, shift0, rwkv0, ret0, attend):
    b, t_len, d = x.shape
    n = b * t_len
    sh1, sc1, gt1, sh2, sc2, gt2 = mods
    x2 = x.reshape(n, d)
    p2 = norm_proj(x2, t_len, lw['norm1_g'], sc1, sh1, lw['w_in'])
    p3 = p2.reshape(b, t_len, PROJ_W)
    prev = jnp.pad(shift0, ((0, 0), (0, RWKV_PAD_W - RWKV_PROJ_W)))[:, None, :]
    r, w, k2, v, kk, kka, g, bv, shift_new = rwkv_prep(p3, prev, lw['mu'], lw['vecs'], lw['lora'],
                                                       consts['seg_ones'])
    n_valid = min(t_len, SCAN_BLOCK)
    tp = -(-t_len // SCAN_BLOCK) * SCAN_BLOCK
    pad_t = lambda a: a if tp == t_len else jnp.pad(a, ((0, 0), (0, tp - t_len), (0, 0)))
    s0t = rwkv0.transpose(0, 3, 1, 2).reshape(b, HEAD_DIM, HW)
    ra, s_fin = rwkv_scan(pad_t(w), pad_t(kk), pad_t(kka), pad_t(k2), pad_t(r), pad_t(v), s0t, n_valid)
    ra = ra[:, :t_len]
    rwkv_new = s_fin.reshape(b, HEAD_DIM, N_HEADS, HEAD_DIM).transpose(0, 2, 3, 1)
    qn, kn, kiw = dsa_prep(p2, lw['qg'], lw['kg'], lw['ig'], lw['ib'], consts['seg_mean'])
    ob = attend(p3, qn, kn, kiw, rel_bias, consts['tri'])
    oc, ret_new = retention(p3, pos0, ret0.reshape(b, HW, RET_DV), lw['ret_gn_g'])
    lw = dict(lw, seg_mean=consts['seg_mean'])
    x1, h2, top_idx, gate = merge(x2, t_len, p2, ra.reshape(n, HW), bv.reshape(n, HW), g.reshape(n, HW),
                                  ob.reshape(n, HW), oc.reshape(n, RET_V_W), gt1, sc2, sh2, lw)
    x_out = moe(x1, t_len, h2, top_idx, gate, gt2, lw)
    return (x_out.reshape(b, t_len, d),
            kn.reshape(b, t_len, N_HEADS, HEAD_DIM),
            p3[:, :, C_V:C_V + HW].reshape(b, t_len, N_HEADS, HEAD_DIM),
            kiw.reshape(b, t_len, LANES)[:, :, :HEAD_DIM],
            rwkv_new,
            shift_new[:, 0, :RWKV_PROJ_W],
            ret_new.reshape(b, N_HEADS, HEAD_DIM, RET_DV))


def kernel(x_prompt, x_sample, c_prompt, c_sample, cache_k, cache_v, cache_kidx, state_rwkv, state_rwkv_shift, state_ret, page_table, rel_bias, ada_w, ada_b, norm1_g, norm2_g, w_in, rwkv_mu, rwkv_w0, rwkv_w2, rwkv_a0, rwkv_a2, rwkv_g2, rwkv_k_k, rwkv_k_a, rwkv_r_k, rwkv_ln_g, rwkv_ln_b, q_norm_g, k_norm_g, idx_k_norm_g, idx_k_norm_b, ret_gn_g, w_up_rwkv, w_up_att, w_up_ret, w_out, router_w, router_b, moe_w1, moe_b1, moe_w2, moe_b2):
    weights = dict(norm1_g=norm1_g, norm2_g=norm2_g, w_in=w_in, rwkv_mu=rwkv_mu, rwkv_w0=rwkv_w0,
                   rwkv_w2=rwkv_w2, rwkv_a0=rwkv_a0, rwkv_a2=rwkv_a2, rwkv_g2=rwkv_g2, rwkv_k_k=rwkv_k_k,
                   rwkv_k_a=rwkv_k_a, rwkv_r_k=rwkv_r_k, rwkv_ln_g=rwkv_ln_g, rwkv_ln_b=rwkv_ln_b,
                   q_norm_g=q_norm_g, k_norm_g=k_norm_g, idx_k_norm_g=idx_k_norm_g,
                   idx_k_norm_b=idx_k_norm_b, ret_gn_g=ret_gn_g, w_up_rwkv=w_up_rwkv, w_up_att=w_up_att,
                   w_up_ret=w_up_ret, w_out=w_out, router_w=router_w, router_b=router_b, moe_w1=moe_w1,
                   moe_b1=moe_b1, moe_w2=moe_w2, moe_b2=moe_b2)
    depth = w_in.shape[0]
    bp, dbatch = x_prompt.shape[0], x_sample.shape[0]
    past = page_table.shape[1] * cache_k.shape[2]
    dt = x_prompt.dtype
    consts = dict(seg_ones=_seg_matrix(1.0), seg_mean=_seg_matrix(1.0 / HEAD_DIM),
                  tri=jnp.asarray(np.triu(np.ones((LANES, LANES), np.float32), 1)).astype(BF16))
    n_c = bp + dbatch
    c_all = jnp.pad(jnp.concatenate([c_prompt, c_sample], axis=0), ((0, -n_c % 8), (0, 0)))
    mods_all = ada_all(c_all, ada_w, ada_b)
    xp, xs = x_prompt, x_sample
    new_p = [[] for _ in range(6)]
    new_s = [[] for _ in range(6)]
    for l in range(depth):
        lw = _pack_layer_weights(l, weights)
        mods_p = jnp.split(mods_all[l, :bp], 6, axis=-1)
        mods_s = jnp.split(mods_all[l, bp:n_c], 6, axis=-1)
        xp, *st_p = _layer_group(xp, mods_p, lw, consts, rel_bias, 0,
                                 jnp.zeros((bp, RWKV_PROJ_W), dt),
                                 jnp.zeros((bp, N_HEADS, HEAD_DIM, HEAD_DIM), dt),
                                 jnp.zeros((bp, N_HEADS, HEAD_DIM, RET_DV), dt),
                                 _attend_prompt)
        attend_s = functools.partial(_attend_sample, cache_k_l=cache_k[l], cache_v_l=cache_v[l],
                                     cache_ki_l=cache_kidx[l], page_table=page_table)
        xs, *st_s = _layer_group(xs, mods_s, lw, consts, rel_bias, past, state_rwkv_shift[l], state_rwkv[l],
                                 state_ret[l], attend_s)
        for lst, a in zip(new_p, st_p):
            lst.append(a)
        for lst, a in zip(new_s, st_s):
            lst.append(a)
    return (xp, xs, *[jnp.stack(a) for a in new_p], *[jnp.stack(a) for a in new_s])
```

```python
import functools
import math

import numpy as np
import jax
import jax.numpy as jnp
from jax import lax
from jax.experimental import pallas as pl
from jax.experimental.pallas import tpu as pltpu

F32 = jnp.float32
BF16 = jnp.bfloat16
I32 = jnp.int32

D_MODEL = 1024
HEAD_DIM = 64
N_HEADS = 8
HW = N_HEADS * HEAD_DIM
LANES = 128
DECAY_LORA, AAA_LORA, GATE_LORA = 64, 64, 160
RWKV_PROJ_W = 3 * HW + DECAY_LORA + AAA_LORA + GATE_LORA
RWKV_PAD_W = 1920
LORA_W = RWKV_PAD_W - 3 * HW
RWKV_GN_EPS = 64e-5
TOPK_KEYS = 256
REL_BUCKETS = 32
REL_MAX_DIST = 128
RET_DV = 2 * HEAD_DIM
RET_V_W = N_HEADS * RET_DV
RET_CHUNK = 128
ROPE_BASE = 10000.0
N_EXPERTS = 32
TOP_K = 4
SWIGLU_LIMIT = 7.0
SWIGLU_ALPHA = 1.702
RMS_EPS = 1e-6
LN_EPS = 1e-5
NEG_INF = -1e30
IN_WIDTHS = (RWKV_PROJ_W, HW, HW, HW, HW, HEAD_DIM, N_HEADS, HW, HW, RET_V_W, RET_V_W, 3 * D_MODEL)

C_RWKV, C_KIWI, C_Q, C_K, C_V, C_QI = 0, 1920, 2048, 2560, 3072, 3584
C_RQ, C_RK, C_RV, C_RG, C_GATE, PROJ_W = 4096, 4608, 5120, 6144, 7168, 10240

VMEM_LIMIT = 56 * 1024 * 1024


def _cparams(*sem):
    return pltpu.CompilerParams(dimension_semantics=sem, vmem_limit_bytes=VMEM_LIMIT)


def _bdot(a, b):
    return jnp.dot(a.astype(BF16), b.astype(BF16), preferred_element_type=F32)


def _bdot_nt(a, b):
    return lax.dot_general(a.astype(BF16), b.astype(BF16), (((1,), (1,)), ((), ())),
                           preferred_element_type=F32)


def _hdot(a, b):
    return jnp.dot(a, b, precision=lax.Precision.HIGHEST, preferred_element_type=F32)


def _split_bf16(a):
    hi = a.astype(BF16)
    return hi, (a - hi.astype(F32)).astype(BF16)


def _mdot(a, b, dims=(((1,), (0,)), ((), ()))):
    ah, al = _split_bf16(a)
    bh, bl = _split_bf16(b)
    d = lambda x, y: lax.dot_general(x, y, dims, preferred_element_type=F32)
    return d(ah, bh) + (d(ah, bl) + d(al, bh))


def _mdot_nt(a, b):
    return _mdot(a, b, (((1,), (1,)), ((), ())))


def _sigmoid(x):
    return 1.0 / (1.0 + jnp.exp(-x))


def _full(shape):
    n = len(shape)
    return pl.BlockSpec(shape, lambda *_: (0,) * n)


def _ada_kernel(c_ref, w_ref, b_ref, o_ref):
    c = c_ref[...]
    o_ref[...] = _bdot(c * _sigmoid(c), w_ref[...]) + b_ref[...]


def ada_all(c_all, ada_w, ada_b):
    n_l, d, d6 = ada_w.shape
    bp = c_all.shape[0]
    tn = 1024
    return pl.pallas_call(
        _ada_kernel,
        grid=(n_l, d6 // tn),
        in_specs=[pl.BlockSpec((bp, d), lambda l, j: (0, 0)),
                  pl.BlockSpec((None, d, tn), lambda l, j: (l, 0, j)),
                  pl.BlockSpec((None, 1, tn), lambda l, j: (l, 0, j))],
        out_specs=pl.BlockSpec((None, bp, tn), lambda l, j: (l, 0, j)),
        out_shape=jax.ShapeDtypeStruct((n_l, bp, d6), F32),
        compiler_params=_cparams("parallel", "parallel"),
        name="ada",
    )(c_all, ada_w, ada_b.reshape(n_l, 1, d6))


def _mod_rows(m, t_len, tm):
    b, d = m.shape
    if t_len % tm == 0:
        per = t_len // tm
        return m[:, None, :], pl.BlockSpec((None, 1, d), lambda i, *_: (i // per, 0, 0))
    assert tm % t_len == 0 and (b * t_len) % tm == 0
    arr = jnp.repeat(m, t_len, axis=0).reshape(b * t_len // tm, tm, d)
    return arr, pl.BlockSpec((None, tm, d), lambda i, *_: (i, 0, 0))


def _norm_proj_kernel(x_ref, g_ref, sc_ref, sh_ref, w_ref, o_ref, h_scr):
    @pl.when(pl.program_id(1) == 0)
    def _():
        x = x_ref[...]
        y = x * lax.rsqrt(jnp.mean(x * x, axis=-1, keepdims=True) + RMS_EPS) * g_ref[...]
        h_scr[...] = (y * (1.0 + sc_ref[...]) + sh_ref[...]).astype(BF16)

    o_ref[...] = jnp.dot(h_scr[...], w_ref[...], preferred_element_type=F32)


def norm_proj(x2, t_len, g, sc, sh, w):
    n, d = x2.shape
    pw = w.shape[1]
    tm = min(n, 1024)
    tn = 1024
    sc3, sc_spec = _mod_rows(sc, t_len, tm)
    sh3, sh_spec = _mod_rows(sh, t_len, tm)
    return pl.pallas_call(
        _norm_proj_kernel,
        grid=(n // tm, pw // tn),
        in_specs=[pl.BlockSpec((tm, d), lambda i, j: (i, 0)),
                  pl.BlockSpec((1, d), lambda i, j: (0, 0)),
                  sc_spec, sh_spec,
                  pl.BlockSpec((d, tn), lambda i, j: (0, j))],
        out_specs=pl.BlockSpec((tm, tn), lambda i, j: (i, j)),
        out_shape=jax.ShapeDtypeStruct((n, pw), F32),
        scratch_shapes=[pltpu.VMEM((tm, d), BF16)],
        compiler_params=_cparams("parallel", "arbitrary"),
        name="norm_proj",
    )(x2, g.reshape(1, d), sc3, sh3, w)


def _rwkv_prep_kernel(p_ref, prev_ref, mu_ref, vec_ref, lw_ref, e_ref,
                      r_o, w_o, k_o, v_o, kk_o, kka_o, g_o, bv_o, shift_o, carry):
    j = pl.program_id(1)
    p = p_ref[...]
    tt = p.shape[0]
    first = jnp.where(j == 0, prev_ref[...], carry[...])
    rows = lax.broadcasted_iota(I32, p.shape, 0)
    p_prev = jnp.where(rows == 0, first, pltpu.roll(p, 1, 0))
    carry[...] = p[tt - 1:tt, :]
    shift_o[...] = p[tt - 1:tt, :]
    xm = p + (p_prev - p) * mu_ref[...]
    r, k, v, xl = xm[:, 0:HW], xm[:, HW:2 * HW], xm[:, 2 * HW:3 * HW], xm[:, 3 * HW:]
    w0, a0, k_k = vec_ref[0:1, :], vec_ref[1:2, :], vec_ref[2:3, :]
    k_a, r_k = vec_ref[3:4, :], vec_ref[4:5, :]
    z = -(w0 + _bdot(jnp.tanh(xl), lw_ref[0]))
    softplus = jnp.maximum(z, 0.0) + jnp.log(1.0 + jnp.exp(-jnp.abs(z)))
    log_decay = -jnp.exp(-softplus - 0.5)
    a = _sigmoid(a0 + _bdot(xl, lw_ref[1]))
    g = _bdot(_sigmoid(xl), lw_ref[2])
    seg_ones = e_ref[...]
    kk = k * k_k
    kk = kk / jnp.maximum(jnp.sqrt(_hdot(kk * kk, seg_ones)), 1e-12)
    k2 = k * (1.0 + (a - 1.0) * k_a)
    r_o[...] = r
    w_o[...] = log_decay
    k_o[...] = k2
    v_o[...] = v
    kk_o[...] = kk
    kka_o[...] = kk * a
    g_o[...] = g
    bv_o[...] = _hdot(r * k2 * r_k, seg_ones) * v


def rwkv_prep(p3, prev, mu, vecs, lora_w, seg_ones):
    b, t_len, _ = p3.shape
    tt = min(t_len, 256)
    tok = pl.BlockSpec((None, tt, HW), lambda i, j: (i, j, 0))
    row = pl.BlockSpec((None, 1, RWKV_PAD_W), lambda i, j: (i, 0, 0))
    tok_shape = jax.ShapeDtypeStruct((b, t_len, HW), F32)
    return pl.pallas_call(
        _rwkv_prep_kernel,
        grid=(b, t_len // tt),
        in_specs=[pl.BlockSpec((None, tt, RWKV_PAD_W), lambda i, j: (i, j, 0)), row,
                  _full((1, RWKV_PAD_W)), _full((8, HW)), _full((3, LORA_W, HW)), _full((HW, HW))],
        out_specs=[tok] * 8 + [row],
        out_shape=[tok_shape] * 8 + [jax.ShapeDtypeStruct((b, 1, RWKV_PAD_W), F32)],
        scratch_shapes=[pltpu.VMEM((1, RWKV_PAD_W), F32)],
        compiler_params=_cparams("parallel", "arbitrary"),
        name="rwkv_prep",
    )(p3, prev, mu, vecs, lora_w, seg_ones)


RWKV_CHUNK = 64
SOLVE_BLOCK = 16


def _rwkv_chunk_kernel(lw_ref, kk_ref, kka_ref, k_ref, r_ref, v_ref, s0_ref, tril_ref, o_ref, sfin_ref, st):
    c = pl.program_id(1)

    @pl.when(c == 0)
    def _():
        st[...] = s0_ref[...]

    n = RWKV_CHUNK
    lw = lw_ref[...]
    lc = _hdot(tril_ref[...], lw)
    lc_last = lc[n - 1:n, :]
    p_inv = jnp.exp(-lc)
    p_rem = jnp.exp(lc_last - lc)
    a_hat = -kk_ref[...] * jnp.exp(lc - lw)
    r_hat = r_ref[...] * jnp.exp(lc)
    b_hat = kka_ref[...] * p_inv
    k_hat = k_ref[...] * p_inv
    b_til = kka_ref[...] * p_rem
    k_til = k_ref[...] * p_rem
    v = v_ref[...]
    p_last = jnp.exp(lc_last)
    row = lax.broadcasted_iota(I32, (2 * n, 2 * n), 0)
    col = lax.broadcasted_iota(I32, (2 * n, 2 * n), 1)
    strict = (col % n) < (row % n)
    incl = (col % n) <= (row % n)
    same_block = (row // SOLVE_BLOCK) == (col // SOLVE_BLOCK)
    same_head = (row // n) == (col // n)
    eye = jnp.where(row == col, 1.0, 0.0)
    low = lax.broadcasted_iota(I32, (n, LANES), 1) < HEAD_DIM
    for p in range(HW // LANES):
        lanes = slice(p * LANES, (p + 1) * LANES)

        def stacked(x):
            xp = x[:, lanes]
            return jnp.concatenate([jnp.where(low, xp, 0.0), jnp.where(low, 0.0, xp)], axis=0)

        ar = jnp.concatenate([stacked(a_hat), stacked(r_hat)], axis=0)
        vs = stacked(v)
        gb = _bdot_nt(ar, stacked(b_hat))
        gk = _bdot_nt(ar, stacked(k_hat))
        l_ab = jnp.where(strict, gb[0:2 * n], 0.0)
        l_ak = jnp.where(strict, gk[0:2 * n], 0.0)
        n_rb = jnp.where(incl, gb[2 * n:4 * n], 0.0)
        n_rk = jnp.where(incl, gk[2 * n:4 * n], 0.0)
        state = st[p]
        t0 = _bdot_nt(ar, state)
        rhs = t0[0:2 * n] + _bdot(l_ak, vs)
        l_d = jnp.where(same_block, l_ab, 0.0)
        l_o = l_ab - l_d
        l_2 = _mdot(l_d, l_d)
        q = eye + l_d
        q = q + _mdot(q, l_2)
        l_4 = _mdot(l_2, l_2)
        q = q + _mdot(q, l_4)
        q = q + _mdot(q, _mdot(l_4, l_4))
        w_o = _mdot(q, l_o)
        z = _mdot(q, rhs)
        y = z + _mdot(_mdot(w_o, w_o), z)
        sa = y + _mdot(w_o, y)
        out = t0[2 * n:4 * n] + _bdot(jnp.concatenate([n_rb, n_rk], axis=1), jnp.concatenate([sa, vs], axis=0))
        o_ref[:, lanes] = out[0:n] + out[n:2 * n]
        x = jnp.concatenate([sa[0:n] + sa[n:2 * n], v[:, lanes]], axis=0)
        upd = _bdot(x.T, jnp.concatenate([b_til[:, lanes], k_til[:, lanes]], axis=0))
        st[p] = state * p_last[:, lanes] + jnp.where(same_head, upd, 0.0)
    sfin_ref[...] = st[...]


def rwkv_chunked(lw, kk, kka, k2, r, v, s0_bd):
    b, tp, _ = lw.shape
    n_pair = HW // LANES
    tok = pl.BlockSpec((None, RWKV_CHUNK, HW), lambda i, j: (i, j, 0))
    state = pl.BlockSpec((None, n_pair, LANES, LANES), lambda i, j: (i, 0, 0, 0))
    tril = jnp.asarray(np.tril(np.ones((RWKV_CHUNK, RWKV_CHUNK), np.float32)))
    return pl.pallas_call(
        _rwkv_chunk_kernel,
        grid=(b, tp // RWKV_CHUNK),
        in_specs=[tok] * 6 + [state, _full((RWKV_CHUNK, RWKV_CHUNK))],
        out_specs=[tok, state],
        out_shape=[jax.ShapeDtypeStruct((b, tp, HW), F32), jax.ShapeDtypeStruct((b, n_pair, LANES, LANES), F32)],
        scratch_shapes=[pltpu.VMEM((n_pair, LANES, LANES), F32)],
        compiler_params=_cparams("parallel", "arbitrary"),
        name="rwkv_chunked",
    )(lw, kk, kka, k2, r, v, s0_bd, tril)


def _dsa_prep_kernel(kiwi_ref, qk_ref, qg_ref, kg_ref, ig_ref, ib_ref, e_ref, qn_o, kn_o, kiw_o):
    seg_mean = e_ref[...]
    q, k = qk_ref[:, 0:HW], qk_ref[:, HW:2 * HW]
    qn_o[...] = q * lax.rsqrt(_hdot(q * q, seg_mean) + RMS_EPS) * qg_ref[...]
    kn_o[...] = k * lax.rsqrt(_hdot(k * k, seg_mean) + RMS_EPS) * kg_ref[...]
    x = kiwi_ref[...]
    is_key = lax.broadcasted_iota(I32, x.shape, 1) < HEAD_DIM
    mu = jnp.sum(jnp.where(is_key, x, 0.0), axis=-1, keepdims=True) / HEAD_DIM
    dev = jnp.where(is_key, x - mu, 0.0)
    var = jnp.sum(dev * dev, axis=-1, keepdims=True) / HEAD_DIM
    kin = dev * lax.rsqrt(var + LN_EPS) * ig_ref[...] + ib_ref[...]
    kiw_o[...] = jnp.where(is_key, kin, x * N_HEADS ** -0.5)


def dsa_prep(p2, qg, kg, ig, ib, seg_mean):
    n = p2.shape[0]
    tm = min(n, 512)
    return pl.pallas_call(
        _dsa_prep_kernel,
        grid=(n // tm,),
        in_specs=[pl.BlockSpec((tm, LANES), lambda i: (i, C_KIWI // LANES)),
                  pl.BlockSpec((tm, 2 * HW), lambda i: (i, C_Q // (2 * HW))),
                  _full((1, HW)), _full((1, HW)), _full((1, LANES)), _full((1, LANES)), _full((HW, HW))],
        out_specs=[pl.BlockSpec((tm, HW), lambda i: (i, 0)), pl.BlockSpec((tm, HW), lambda i: (i, 0)),
                   pl.BlockSpec((tm, LANES), lambda i: (i, 0))],
        out_shape=[jax.ShapeDtypeStruct((n, HW), F32), jax.ShapeDtypeStruct((n, HW), F32),
                   jax.ShapeDtypeStruct((n, LANES), F32)],
        compiler_params=_cparams("parallel"),
        name="dsa_prep",
    )(p2, p2, qg, kg, ig, ib, seg_mean)


INT_MIN = -2 ** 31
SEL_CHUNK = 512


def _order_key(score):
    bits = lax.bitcast_convert_type(score + 0.0, I32)
    return bits ^ ((bits >> 31) & 0x7FFFFFFF)


def _count_ge(key_ref, n_chunks, cand, rows, strict=False):
    def body(c, acc):
        blk = key_ref[:, pl.ds(pl.multiple_of(c * SEL_CHUNK, SEL_CHUNK), SEL_CHUNK)]
        hit = (blk > cand) if strict else (blk >= cand)
        ones = jnp.where(hit, 1.0, 0.0)
        for g in range(SEL_CHUNK // LANES):
            acc = acc + ones[:, g * LANES:(g + 1) * LANES]
        return acc
    acc = lax.fori_loop(0, n_chunks, body, jnp.zeros((rows, LANES), F32))
    return jnp.sum(acc, axis=1, keepdims=True)


def _kth_largest_key(key_ref, n_chunks, n_sel, rows):
    cnt0 = _count_ge(key_ref, n_chunks, jnp.zeros((rows, 1), I32), rows)
    thr = jnp.where(cnt0 >= n_sel, 0, INT_MIN).astype(I32)

    def bit_body(it, thr):
        cand = thr + jnp.left_shift(jnp.int32(1), 30 - it)
        cnt = _count_ge(key_ref, n_chunks, cand, rows)
        return jnp.where(cnt >= n_sel, cand, thr)

    return lax.fori_loop(0, 31, bit_body, thr)


def _select_into(key_ref, valid_fn, n_chunks, n_sel, rows, tri_ref, write_fn):
    thr = _kth_largest_key(key_ref, n_chunks, n_sel, rows)
    n_gt = _count_ge(key_ref, n_chunks, thr, rows, strict=True)
    need = n_sel - n_gt

    def chunk(c):
        return key_ref[:, pl.ds(pl.multiple_of(c * SEL_CHUNK, SEL_CHUNK), SEL_CHUNK)]

    def count_ties(c, acc):
        return acc + jnp.sum(jnp.where((chunk(c) == thr) & valid_fn(c), 1.0, 0.0), axis=1, keepdims=True)

    n_eq = lax.fori_loop(0, n_chunks, count_ties, jnp.zeros((rows, 1), F32))
    surplus = jnp.max(n_eq - need) > 0.0

    @pl.when(jnp.logical_not(surplus))
    def _():
        def body(c, carry):
            blk = chunk(c)
            write_fn(c, (blk >= thr) & valid_fn(c))
            return carry
        lax.fori_loop(0, n_chunks, body, 0)

    @pl.when(surplus)
    def _():
        def body(c, seen):
            blk = chunk(c)
            val = valid_fn(c)
            tie = jnp.where((blk == thr) & val, 1.0, 0.0)
            parts = []
            for g in range(SEL_CHUNK // LANES):
                tg = tie[:, g * LANES:(g + 1) * LANES]
                before = _bdot(tg, tri_ref[...]) + seen
                parts.append(jnp.where(before < need, tg, 0.0))
                seen = seen + jnp.sum(tg, axis=1, keepdims=True)
            take = jnp.concatenate(parts, axis=1) > 0.0
            write_fn(c, ((blk > thr) & val) | take)
            return seen
        lax.fori_loop(0, n_chunks, body, jnp.zeros((rows, 1), F32))


def _dsa_select_kernel(qi_ref, ki_ref, wi_ref, tri_ref, m_ref, key_scr, *, n_sel, qb):
    i = pl.program_id(1)
    n_chunks = (i * qb + qb + SEL_CHUNK - 1) // SEL_CHUNK
    m_ref[...] = jnp.zeros(m_ref.shape, m_ref.dtype)
    qpos = i * qb + lax.broadcasted_iota(I32, (qb, SEL_CHUNK), 0)
    lane_pos = lax.broadcasted_iota(I32, (qb, SEL_CHUNK), 1)
    wi = wi_ref[...]

    def causal(c):
        return lane_pos + c * SEL_CHUNK <= qpos

    def score_chunk(c, carry):
        off = pl.multiple_of(c * SEL_CHUNK, SEL_CHUNK)
        s = jnp.maximum(_bdot_nt(qi_ref[...], ki_ref[pl.ds(off, SEL_CHUNK), :]), 0.0)
        acc = jnp.zeros((qb, SEL_CHUNK), F32)
        for h in range(N_HEADS):
            acc = acc + s[h * qb:(h + 1) * qb, :] * wi[:, HEAD_DIM + h:HEAD_DIM + h + 1]
        key_scr[:, pl.ds(off, SEL_CHUNK)] = _order_key(jnp.where(causal(c), acc, NEG_INF))
        return carry

    lax.fori_loop(0, n_chunks, score_chunk, 0)

    def write(c, sel):
        off = pl.multiple_of(c * SEL_CHUNK, SEL_CHUNK)
        m_ref[:, pl.ds(off, SEL_CHUNK)] = jnp.where(sel, 1, 0).astype(m_ref.dtype)

    _select_into(key_scr, causal, n_chunks, n_sel, qb, tri_ref, write)


def dsa_select(qi_stack, ki, kiw3, tri, n_sel):
    b, nb, rows, _ = qi_stack.shape
    qb = rows // N_HEADS
    s_len = ki.shape[1]
    return pl.pallas_call(
        functools.partial(_dsa_select_kernel, n_sel=n_sel, qb=qb),
        grid=(b, nb),
        in_specs=[pl.BlockSpec((None, None, rows, HEAD_DIM), lambda bi, i: (bi, i, 0, 0)),
                  pl.BlockSpec((None, s_len, HEAD_DIM), lambda bi, i: (bi, 0, 0)),
                  pl.BlockSpec((None, qb, LANES), lambda bi, i: (bi, i, 0)),
                  _full((LANES, LANES))],
        out_specs=pl.BlockSpec((None, qb, s_len), lambda bi, i: (bi, i, 0)),
        out_shape=jax.ShapeDtypeStruct((b, s_len, s_len), jnp.int8),
        scratch_shapes=[pltpu.VMEM((qb, s_len), I32)],
        compiler_params=_cparams("parallel", "arbitrary"),
        name="dsa_select",
    )(qi_stack, ki, kiw3, tri)


ATT_CHUNK = 512
ATT_NEAR = 256
M_INIT = -1e29


def _softmax_step(carry, s, v):
    m, l, acc = carry
    m_new = jnp.maximum(m, jnp.max(s, axis=1, keepdims=True))
    alpha = jnp.exp(m - m_new)
    p = jnp.exp(s - m_new)
    return m_new, l * alpha + jnp.sum(p, axis=1, keepdims=True), acc * alpha + _bdot(p, v)


def _dsa_attn_kernel(q_ref, k_ref, v_ref, m_ref, toe_ref, o_ref, *, qb):
    i = pl.program_id(1)
    near_start = jnp.maximum(i - 1, 0) * qb
    n_far = (near_start + ATT_CHUNK - 1) // ATT_CHUNK
    n_pair = HW // LANES
    low = lax.broadcasted_iota(I32, (qb, LANES), 1) < HEAD_DIM
    kpos = lax.broadcasted_iota(I32, (qb, ATT_CHUNK), 1)
    q = q_ref[...]
    q2 = []
    for p in range(n_pair):
        qp = q[:, p * LANES:(p + 1) * LANES]
        zero = jnp.zeros_like(qp)
        q2.append(jnp.concatenate([jnp.where(low, qp, zero), jnp.where(low, zero, qp)], axis=0))

    def far_chunk(c, carry):
        off = pl.multiple_of(c * ATT_CHUNK, ATT_CHUNK)
        sel = m_ref[:, pl.ds(off, ATT_CHUNK)].astype(I32) != 0
        add = jnp.where(sel & (kpos + off < near_start), 0.0, NEG_INF)
        add2 = jnp.concatenate([add, add], axis=0)
        out = []
        for p in range(n_pair):
            lanes = slice(p * LANES, (p + 1) * LANES)
            s = _bdot_nt(q2[p], k_ref[pl.ds(off, ATT_CHUNK), lanes]) + add2
            out.append(_softmax_step(carry[p], s, v_ref[pl.ds(off, ATT_CHUNK), lanes]))
        return tuple(out)

    init = tuple((jnp.full((2 * qb, 1), M_INIT, F32), jnp.zeros((2 * qb, 1), F32),
                  jnp.zeros((2 * qb, LANES), F32)) for _ in range(n_pair))
    carry = lax.fori_loop(0, n_far, far_chunk, init)
    off = pl.multiple_of(near_start, qb)
    add = jnp.where(m_ref[:, pl.ds(off, ATT_NEAR)].astype(I32) != 0, 0.0, NEG_INF)
    for p in range(n_pair):
        lanes = slice(p * LANES, (p + 1) * LANES)
        bias2 = jnp.concatenate([toe_ref[2 * p] + add, toe_ref[2 * p + 1] + add], axis=0)
        s = _bdot_nt(q2[p], k_ref[pl.ds(off, ATT_NEAR), lanes]) + bias2
        _, l, acc = _softmax_step(carry[p], s, v_ref[pl.ds(off, ATT_NEAR), lanes])
        o = acc / l
        o_ref[:, lanes] = jnp.where(low, o[0:qb, :], o[qb:2 * qb, :])


def dsa_attn(q, k, v, mask, toe):
    b, s_len, _ = q.shape
    qb = 128
    kv = pl.BlockSpec((None, s_len, HW), lambda bi, i: (bi, 0, 0))
    return pl.pallas_call(
        functools.partial(_dsa_attn_kernel, qb=qb),
        grid=(b, s_len // qb),
        in_specs=[pl.BlockSpec((None, qb, HW), lambda bi, i: (bi, i, 0)), kv, kv,
                  pl.BlockSpec((None, qb, s_len), lambda bi, i: (bi, i, 0)),
                  pl.BlockSpec((None, N_HEADS, qb, ATT_NEAR), lambda bi, i: (jnp.minimum(i, 1), 0, 0, 0))],
        out_specs=pl.BlockSpec((None, qb, HW), lambda bi, i: (bi, i, 0)),
        out_shape=jax.ShapeDtypeStruct((b, s_len, HW), F32),
        compiler_params=_cparams("parallel", "arbitrary"),
        name="dsa_attn",
    )(q, k, v, mask, toe)


def _dsa_sample_kernel(pt_ref, qi_ref, wcol_ref, q_ref, *refs, n_pages, n_sel, t_new, page, group):
    cki, nki_ref = refs[0:group], refs[group]
    ck, nk_ref = refs[group + 1:2 * group + 1], refs[2 * group + 1]
    cv, nv_ref = refs[2 * group + 2:3 * group + 2], refs[3 * group + 2]
    bias_ref, tri_ref, o_ref, key_scr, sel_scr, m_scr, l_scr, acc_scr = refs[3 * group + 3:]
    phase = pl.program_id(1)
    g = pl.program_id(2)
    rows = N_HEADS * t_new
    gw = group * page
    new_off = n_pages * page
    n_chunks = -(-(n_pages + 1) * page // SEL_CHUNK)
    last_group = g == n_pages // group - 1

    def index_scores(ki):
        s = jnp.maximum(_bdot_nt(qi_ref[...], ki), 0.0) * wcol_ref[:, 0:1]
        acc = jnp.zeros((t_new, s.shape[1]), F32)
        for h in range(N_HEADS):
            acc = acc + s[h * t_new:(h + 1) * t_new, :]
        return acc

    @pl.when(jnp.logical_and(phase == 0, g == 0))
    def _():
        key_scr[...] = jnp.full(key_scr.shape, INT_MIN, I32)

    @pl.when(phase == 0)
    def _():
        ki = jnp.concatenate([r[...].astype(BF16) for r in cki], axis=0)
        key_scr[:, pl.ds(pl.multiple_of(g * gw, gw), gw)] = _order_key(index_scores(ki))

        @pl.when(last_group)
        def _():
            ok = lax.broadcasted_iota(I32, (t_new, page), 1) <= lax.broadcasted_iota(I32, (t_new, page), 0)
            key_scr[:, new_off:new_off + page] = _order_key(jnp.where(ok, index_scores(nki_ref[...]), NEG_INF))

    @pl.when(jnp.logical_and(phase == 1, g == 0))
    def _():
        pos = lax.broadcasted_iota(I32, (t_new, SEL_CHUNK), 1)
        qpos = new_off + lax.broadcasted_iota(I32, (t_new, SEL_CHUNK), 0)

        def valid(c):
            return pos + c * SEL_CHUNK <= qpos

        def write(c, sel):
            sel_scr[:, pl.ds(pl.multiple_of(c * SEL_CHUNK, SEL_CHUNK), SEL_CHUNK)] = jnp.where(sel, 0.0, NEG_INF)

        _select_into(key_scr, valid, n_chunks, n_sel, t_new, tri_ref, write)
        m_scr[...] = jnp.full(m_scr.shape, M_INIT, F32)
        l_scr[...] = jnp.zeros(l_scr.shape, F32)
        acc_scr[...] = jnp.zeros(acc_scr.shape, F32)

    @pl.when(phase == 1)
    def _():
        head_of_row = lax.broadcasted_iota(I32, (rows, HW), 0) // t_new
        own_head = head_of_row == lax.broadcasted_iota(I32, (rows, HW), 1) // HEAD_DIM
        qx = jnp.where(own_head, jnp.concatenate([q_ref[...]] * N_HEADS, axis=0), 0.0)

        def attend(k, v, add_rows, bias):
            s = _bdot_nt(qx, k) + jnp.concatenate([add_rows] * N_HEADS, axis=0)
            if bias is not None:
                s = s + bias
            m, l, acc = _softmax_step((m_scr[...], l_scr[...], acc_scr[...]), s, v)
            m_scr[...] = m
            l_scr[...] = l
            acc_scr[...] = acc

        k = jnp.concatenate([r[...].astype(BF16) for r in ck], axis=0)
        v = jnp.concatenate([r[...].astype(BF16) for r in cv], axis=0)
        near = jnp.where(last_group, bias_ref[0], 0.0)
        bias = jnp.concatenate([jnp.zeros((rows, gw - page), F32), near], axis=1) if gw > page else near
        attend(k, v, sel_scr[:, pl.ds(pl.multiple_of(g * gw, gw), gw)], bias)

        @pl.when(last_group)
        def _():
            attend(nk_ref[...], nv_ref[...], sel_scr[:, new_off:new_off + page], bias_ref[1])
            o_full = jnp.where(own_head, acc_scr[...] / l_scr[...], 0.0)
            out = jnp.zeros((t_new, HW), F32)
            for h in range(N_HEADS):
                out = out + o_full[h * t_new:(h + 1) * t_new, :]
            o_ref[...] = out


def dsa_sample(page_table, qi_rows, wcol, q3, cache_ki, new_ki, cache_k, new_k, cache_v, new_v, bias_tab, tri,
               n_sel):
    db, n_pages = page_table.shape
    page = cache_ki.shape[1]
    t_new = q3.shape[1]
    rows = N_HEADS * t_new
    group = math.gcd(n_pages, 8)
    n_groups = n_pages // group
    n_chunks = -(-(n_pages + 1) * page // SEL_CHUNK)

    def cache_spec(width, active_phase, idle_group, u):
        def index(b, ph, g, pt):
            use = jnp.where(ph == active_phase, g, idle_group) * group + u
            return (pt[b * n_pages + use], 0, 0)
        return pl.BlockSpec((None, page, width), index)

    def per_b(shape):
        return pl.BlockSpec((None,) + shape, lambda b, ph, g, pt: (b, 0, 0))

    ki_specs = [cache_spec(HEAD_DIM, 0, n_groups - 1, u) for u in range(group)]
    kv_specs = [cache_spec(HW, 1, 0, u) for u in range(group)]
    grid_spec = pltpu.PrefetchScalarGridSpec(
        num_scalar_prefetch=1,
        grid=(db, 2, n_groups),
        in_specs=[per_b((rows, HEAD_DIM)), per_b((rows, LANES)), per_b((t_new, HW))]
                 + ki_specs + [per_b((page, HEAD_DIM))] + kv_specs + [per_b((page, HW))]
                 + kv_specs + [per_b((page, HW))]
                 + [pl.BlockSpec((2, rows, page), lambda b, ph, g, pt: (0, 0, 0)),
                    pl.BlockSpec((LANES, LANES), lambda b, ph, g, pt: (0, 0))],
        out_specs=pl.BlockSpec((None, t_new, HW), lambda b, ph, g, pt: (b, 0, 0)),
        scratch_shapes=[pltpu.VMEM((t_new, n_chunks * SEL_CHUNK), I32),
                        pltpu.VMEM((t_new, n_chunks * SEL_CHUNK), F32),
                        pltpu.VMEM((rows, 1), F32), pltpu.VMEM((rows, 1), F32), pltpu.VMEM((rows, HW), F32)])
    return pl.pallas_call(
        functools.partial(_dsa_sample_kernel, n_pages=n_pages, n_sel=n_sel, t_new=t_new, page=page, group=group),
        grid_spec=grid_spec,
        out_shape=jax.ShapeDtypeStruct((db, t_new, HW), F32),
        compiler_params=_cparams("parallel", "arbitrary", "arbitrary"),
        name="dsa_sample",
    )(page_table.reshape(-1), qi_rows, wcol, q3, *([cache_ki] * group), new_ki, *([cache_k] * group), new_k,
      *([cache_v] * group), new_v, bias_tab, tri)


def _retention_kernel(q_ref, k_ref, v_ref, g_ref, cos_ref, sin_ref, dmask_ref, qdec_ref, kdec_ref, cdec_ref,
                      gn_ref, s0_ref, o_ref, sfin_ref, st, *, rows):
    c = pl.program_id(1)

    @pl.when(c == 0)
    def _():
        st[...] = s0_ref[...]

    def padded(x):
        if x.shape[0] == RET_CHUNK:
            return x
        return jnp.concatenate([x, jnp.zeros((RET_CHUNK - x.shape[0], x.shape[1]), x.dtype)], axis=0)

    lane = lax.broadcasted_iota(I32, (RET_CHUNK, HW), 1)
    first_half = (lane % HEAD_DIM) < HEAD_DIM // 2

    def rope(x):
        partner = jnp.where(first_half, pltpu.roll(x, HW - HEAD_DIM // 2, 1), pltpu.roll(x, HEAD_DIM // 2, 1))
        return x * padded(cos_ref[...]) + partner * padded(sin_ref[...])

    q = rope(padded(q_ref[...]))
    k = rope(padded(k_ref[...])) * HEAD_DIM ** -0.5
    v = padded(v_ref[...])
    state = st[...]
    head_of_lane = lane // HEAD_DIM
    qdec = qdec_ref[...]
    o_parts = []
    for h in range(N_HEADS):
        qh = jnp.where(head_of_lane == h, q, 0.0)
        inner = _bdot_nt(qh, k) * dmask_ref[h]
        vh = v[:, h * RET_DV:(h + 1) * RET_DV]
        oh = _bdot(inner, vh) + _bdot(qh, state) * qdec[:, h:h + 1]
        mu = jnp.mean(oh, axis=-1, keepdims=True)
        dev = oh - mu
        var = jnp.mean(dev * dev, axis=-1, keepdims=True)
        o_parts.append(dev * lax.rsqrt(var + LN_EPS))
    o = jnp.concatenate(o_parts, axis=1) * gn_ref[...]
    gate = g_ref[...]
    o_ref[...] = gate * _sigmoid(gate) * o[0:rows, :]
    kv = _bdot((k * kdec_ref[...]).T, v)
    upd = jnp.concatenate([kv[h * HEAD_DIM:(h + 1) * HEAD_DIM, h * RET_DV:(h + 1) * RET_DV]
                           for h in range(N_HEADS)], axis=0)
    st[...] = state * cdec_ref[...] + upd
    sfin_ref[...] = st[...]


def retention(p3, pos0, s0, gn_g):
    b, t_len, _ = p3.shape
    chunk = math.gcd(t_len, RET_CHUNK)
    n_chunk = t_len // chunk
    half = HEAD_DIM // 2
    inv = ROPE_BASE ** (-jnp.arange(half, dtype=F32) / half)
    ang = (pos0 + jnp.arange(t_len)).astype(F32)[:, None] * inv[None, :]
    cos = jnp.tile(jnp.cos(ang), (1, 2 * N_HEADS))
    sin = jnp.tile(jnp.concatenate([-jnp.sin(ang), jnp.sin(ang)], axis=1), (1, N_HEADS))
    log_g = jnp.log(1.0 - 2.0 ** (-5.0 - jnp.arange(N_HEADS, dtype=F32)))
    idx = jnp.arange(RET_CHUNK, dtype=F32)
    rel = idx[:, None] - idx[None, :]
    dmask = jnp.where(rel[None] >= 0, jnp.exp(rel[None] * log_g[:, None, None]), 0.0)
    qdec = jnp.pad(jnp.exp((idx + 1.0)[:, None] * log_g[None, :]), ((0, 0), (0, LANES - N_HEADS)))
    kdec = jnp.where((idx < chunk)[:, None], jnp.exp((chunk - 1.0 - idx)[:, None] * log_g[None, :]), 0.0)
    kdec = jnp.repeat(kdec, HEAD_DIM, axis=1)
    cdec = jnp.broadcast_to(jnp.repeat(jnp.exp(chunk * log_g), HEAD_DIM)[:, None], (HW, RET_DV))
    tok = lambda width, col: pl.BlockSpec((None, chunk, width), lambda i, j: (i, j, col))
    state = pl.BlockSpec((None, HW, RET_DV), lambda i, j: (i, 0, 0))
    return pl.pallas_call(
        functools.partial(_retention_kernel, rows=chunk),
        grid=(b, n_chunk),
        in_specs=[tok(HW, C_RQ // HW), tok(HW, C_RK // HW), tok(RET_V_W, C_RV // RET_V_W),
                  tok(RET_V_W, C_RG // RET_V_W),
                  pl.BlockSpec((chunk, HW), lambda i, j: (j, 0)), pl.BlockSpec((chunk, HW), lambda i, j: (j, 0)),
                  _full((N_HEADS, RET_CHUNK, RET_CHUNK)), _full((RET_CHUNK, LANES)), _full((RET_CHUNK, HW)),
                  _full((HW, RET_DV)), _full((1, RET_V_W)), state],
        out_specs=[pl.BlockSpec((None, chunk, RET_V_W), lambda i, j: (i, j, 0)), state],
        out_shape=[jax.ShapeDtypeStruct((b, t_len, RET_V_W), F32), jax.ShapeDtypeStruct((b, HW, RET_DV), F32)],
        scratch_shapes=[pltpu.VMEM((HW, RET_DV), F32)],
        compiler_params=_cparams("parallel", "arbitrary"),
        name="retention",
    )(p3, p3, p3, p3, cos, sin, dmask, qdec, kdec, cdec, gn_g.reshape(1, RET_V_W), s0)


def _merge_kernel(x_ref, ra_ref, bv_ref, g_ref, ob_ref, oc_ref, ga_ref, gb_ref, gc_ref,
                  gt1_ref, sc2_ref, sh2_ref, lng_ref, lnb_ref, e_ref, wa_ref, wb_ref, wc_ref, wo_ref,
                  n2_ref, rw_ref, rb_ref, x1_o, h2_o, idx_o, gate_o):
    seg_mean = e_ref[...]
    ra = ra_ref[...]
    mu = _hdot(ra, seg_mean)
    dev = ra - mu
    var = _hdot(dev * dev, seg_mean)
    oa = (dev * lax.rsqrt(var + RWKV_GN_EPS) * lng_ref[...] + lnb_ref[...] + bv_ref[...]) * g_ref[...]
    merged = (_sigmoid(ga_ref[...]) * _bdot(oa, wa_ref[...])
              + _sigmoid(gb_ref[...]) * _bdot(ob_ref[...], wb_ref[...])
              + _sigmoid(gc_ref[...]) * _bdot(oc_ref[...], wc_ref[...]))
    x1 = x_ref[...] + gt1_ref[...] * _bdot(merged, wo_ref[...])
    x1_o[...] = x1
    y = x1 * lax.rsqrt(jnp.mean(x1 * x1, axis=-1, keepdims=True) + RMS_EPS) * n2_ref[...]
    h2 = y * (1.0 + sc2_ref[...]) + sh2_ref[...]
    h2_o[...] = h2
    logits = _hdot(h2, rw_ref[...]) + rb_ref[...]
    lane = lax.broadcasted_iota(I32, logits.shape, 1).astype(F32)
    idx_acc = jnp.zeros(logits.shape, F32)
    val_acc = jnp.zeros(logits.shape, F32)
    top = None
    denom = jnp.zeros((logits.shape[0], 1), F32)
    for k in range(TOP_K):
        m = jnp.max(logits, axis=-1, keepdims=True)
        ix = jnp.min(jnp.where(logits == m, lane, float(LANES)), axis=-1, keepdims=True)
        top = m if top is None else top
        ev = jnp.exp(m - top)
        denom = denom + ev
        idx_acc = jnp.where(lane == k, ix, idx_acc)
        val_acc = jnp.where(lane == k, ev, val_acc)
        logits = jnp.where(lane == ix, -jnp.inf, logits)
    idx_o[...] = idx_acc.astype(I32)
    gate_o[...] = val_acc / denom


def merge(x2, t_len, p2, ra, bv, g, ob, oc, gt1, sc2, sh2, lw):
    n, d = x2.shape
    tm = min(n, 256)
    mods = [_mod_rows(m, t_len, tm) for m in (gt1, sc2, sh2)]
    row = lambda width, col=0: pl.BlockSpec((tm, width), lambda i: (i, col))
    gate_col = C_GATE // d
    out_shape = [jax.ShapeDtypeStruct((n, d), F32), jax.ShapeDtypeStruct((n, d), F32),
                 jax.ShapeDtypeStruct((n, LANES), I32), jax.ShapeDtypeStruct((n, LANES), F32)]
    return pl.pallas_call(
        _merge_kernel,
        grid=(n // tm,),
        in_specs=[row(d), row(HW), row(HW), row(HW), row(HW), row(RET_V_W),
                  row(d, gate_col), row(d, gate_col + 1), row(d, gate_col + 2)]
                 + [m[1] for m in mods]
                 + [_full((1, HW)), _full((1, HW)), _full((HW, HW)), _full((HW, d)), _full((HW, d)),
                    _full((RET_V_W, d)), _full((d, d)), _full((1, d)), _full((d, LANES)), _full((1, LANES))],
        out_specs=[row(d), row(d), row(LANES), row(LANES)],
        out_shape=out_shape,
        compiler_params=_cparams("parallel"),
        name="merge",
    )(x2, ra, bv, g, ob, oc, p2, p2, p2, *[m[0] for m in mods],
      lw['ln_g'], lw['ln_b'], lw['seg_mean'], lw['w_up_a'], lw['w_up_b'], lw['w_up_c'], lw['w_out'],
      lw['norm2_g'], lw['router_w'], lw['router_b'])


SPLIT_W = 2 * LANES


def _split_even_odd_kernel(w_ref, perm_ref, even_o, odd_o):
    perm = perm_ref[...]
    for c in range(w_ref.shape[1] // SPLIT_W):
        blk = jnp.dot(w_ref[:, c * SPLIT_W:(c + 1) * SPLIT_W].astype(BF16), perm, preferred_element_type=F32)
        even_o[:, c * LANES:(c + 1) * LANES] = blk[:, 0:LANES].astype(BF16)
        odd_o[:, c * LANES:(c + 1) * LANES] = blk[:, LANES:SPLIT_W].astype(BF16)


def split_even_odd(w):
    n_e, d, f2 = w.shape
    tk = 512
    m = np.arange(LANES)
    perm = np.zeros((SPLIT_W, SPLIT_W), np.float32)
    perm[2 * m, m] = 1.0
    perm[2 * m + 1, LANES + m] = 1.0
    out = jax.ShapeDtypeStruct((n_e, d, f2 // 2), BF16)
    return pl.pallas_call(
        _split_even_odd_kernel,
        grid=(n_e, d // tk),
        in_specs=[pl.BlockSpec((None, tk, f2), lambda e, i: (e, i, 0)), _full((SPLIT_W, SPLIT_W))],
        out_specs=[pl.BlockSpec((None, tk, f2 // 2), lambda e, i: (e, i, 0))] * 2,
        out_shape=[out, out],
        compiler_params=_cparams("parallel", "parallel"),
        name="split_even_odd",
    )(w, jnp.asarray(perm).astype(BF16))


def _moe_ffn_kernel(be_ref, used_ref, x_ref, w1g_ref, w1l_ref, b1g_ref, b1l_ref, w2_ref, b2_ref, o_ref):
    i = pl.program_id(0)
    rows = x_ref.shape[0]

    @pl.when(i * rows < used_ref[0])
    def _():
        x = x_ref[...].astype(BF16)
        glu = jnp.minimum(jnp.dot(x, w1g_ref[...], preferred_element_type=F32) + b1g_ref[...], SWIGLU_LIMIT)
        lin = jnp.clip(jnp.dot(x, w1l_ref[...], preferred_element_type=F32) + b1l_ref[...],
                       -SWIGLU_LIMIT, SWIGLU_LIMIT)
        act = glu * _sigmoid(SWIGLU_ALPHA * glu) * (lin + 1.0)
        o_ref[...] = _bdot(act, w2_ref[...]) + b2_ref[...]

    @pl.when(i * rows >= used_ref[0])
    def _():
        o_ref[...] = jnp.zeros(o_ref.shape, F32)


def moe_ffn(block_expert, n_used, x_rows, lw, rows):
    n_rows, d = x_rows.shape
    n_blocks = n_rows // rows
    d_ff = lw['moe_w2'].shape[1]
    wspec = lambda a, b_: pl.BlockSpec((None, a, b_), lambda i, be, nu: (be[i], 0, 0))
    grid_spec = pltpu.PrefetchScalarGridSpec(
        num_scalar_prefetch=2,
        grid=(n_blocks,),
        in_specs=[pl.BlockSpec((rows, d), lambda i, be, nu: (i, 0)),
                  wspec(d, d_ff), wspec(d, d_ff), wspec(1, d_ff), wspec(1, d_ff), wspec(d_ff, d), wspec(1, d)],
        out_specs=pl.BlockSpec((rows, d), lambda i, be, nu: (i, 0)))
    return pl.pallas_call(
        _moe_ffn_kernel,
        grid_spec=grid_spec,
        out_shape=jax.ShapeDtypeStruct((n_rows, d), F32),
        compiler_params=_cparams("arbitrary"),
        name="moe_ffn",
    )(block_expert, n_used, x_rows, lw['moe_w1g'], lw['moe_w1l'], lw['moe_b1g'], lw['moe_b1l'],
      lw['moe_w2'], lw['moe_b2'])


def _combine_kernel(x_ref, y_ref, gate_ref, gt2_ref, o_ref):
    gate = gate_ref[...]
    y = jnp.zeros(x_ref.shape, F32)
    for k in range(TOP_K):
        y = y + gate[:, k:k + 1] * y_ref[k]
    o_ref[...] = x_ref[...] + gt2_ref[...] * y


def combine(x1, t_len, y4, gate, gt2):
    n, d = x1.shape
    tm = min(n, 256)
    gt2_3, gt2_spec = _mod_rows(gt2, t_len, tm)
    return pl.pallas_call(
        _combine_kernel,
        grid=(n // tm,),
        in_specs=[pl.BlockSpec((tm, d), lambda i: (i, 0)), pl.BlockSpec((TOP_K, tm, d), lambda i: (0, i, 0)),
                  pl.BlockSpec((tm, LANES), lambda i: (i, 0)), gt2_spec],
        out_specs=pl.BlockSpec((tm, d), lambda i: (i, 0)),
        out_shape=jax.ShapeDtypeStruct((n, d), F32),
        compiler_params=_cparams("parallel"),
        name="combine",
    )(x1, y4, gate, gt2_3)


def moe(x1, t_len, h2, top_idx, gate, gt2, lw):
    n, d = h2.shape
    rows = 256 if n * TOP_K >= 8192 else 64
    a_n = n * TOP_K
    e_flat = top_idx[:, :TOP_K].reshape(-1)
    onehot = (e_flat[:, None] == jnp.arange(N_EXPERTS, dtype=I32)[None, :]).astype(I32)
    ranks = jnp.cumsum(onehot, axis=0) - onehot
    rank = jnp.sum(ranks * onehot, axis=1)
    counts = jnp.sum(onehot, axis=0)
    padded = (counts + rows - 1) // rows * rows
    pends = jnp.cumsum(padded)
    pstarts = pends - padded
    dest = (pstarts[e_flat] + rank).astype(I32)
    n_blocks = -(-a_n // rows) + N_EXPERTS
    tok_flat = jnp.repeat(jnp.arange(n, dtype=I32), TOP_K)
    row_tok = jnp.zeros(n_blocks * rows, I32).at[dest].set(tok_flat)
    block_expert = jnp.minimum(jnp.searchsorted(pends, jnp.arange(n_blocks, dtype=I32) * rows, side='right'),
                               N_EXPERTS - 1).astype(I32)
    x_rows = h2[row_tok]
    y_rows = moe_ffn(block_expert, pends[-1:].astype(I32), x_rows, lw, rows)
    y4 = y_rows[dest.reshape(n, TOP_K).T]
    return combine(x1, t_len, y4, gate, gt2)


def _t5_bucket(dist):
    max_exact = REL_BUCKETS // 2
    d = jnp.maximum(dist, 0)
    large = max_exact + (jnp.log(jnp.maximum(d, 1).astype(F32) / max_exact)
                         / math.log(REL_MAX_DIST / max_exact) * (REL_BUCKETS - max_exact)).astype(I32)
    large = jnp.minimum(large, REL_BUCKETS - 1)
    return jnp.where(d < max_exact, d, large)


def _seg_matrix(value):
    head = np.arange(HW) // HEAD_DIM
    return jnp.asarray((head[:, None] == head[None, :]).astype(np.float32) * value)


def _pack_layer_weights(l, w):
    d = D_MODEL
    parts = jnp.split(w['w_in'][l], np.cumsum(np.array(IN_WIDTHS))[:-1].tolist(), axis=1)
    p_rwkv, pq, pk, pv, pqi, pki, pwi, rq, rk, rv, rg, pgate = parts
    zeros = lambda n: jnp.zeros((d, n), F32)
    w_in = jnp.concatenate([p_rwkv, zeros(RWKV_PAD_W - RWKV_PROJ_W), pki, pwi, zeros(LANES - HEAD_DIM - N_HEADS),
                            pq, pk, pv, pqi, rq, rk, rv, rg, pgate], axis=1).astype(BF16)
    lora = jnp.zeros((3, LORA_W, HW), F32)
    lora = lora.at[0, 0:DECAY_LORA].set(w['rwkv_w2'][l])
    lora = lora.at[1, DECAY_LORA:DECAY_LORA + AAA_LORA].set(w['rwkv_a2'][l])
    lora = lora.at[2, DECAY_LORA + AAA_LORA:DECAY_LORA + AAA_LORA + GATE_LORA].set(w['rwkv_g2'][l])
    vecs = jnp.zeros((8, HW), F32)
    for i, name in enumerate(('rwkv_w0', 'rwkv_a0', 'rwkv_k_k', 'rwkv_k_a', 'rwkv_r_k')):
        vecs = vecs.at[i].set(w[name][l])
    pad_lanes = lambda v, fill=0.0: jnp.concatenate(
        [v, jnp.full((LANES - v.shape[0],), fill, F32)]).reshape(1, LANES)
    b1 = w['moe_b1'][l]
    return dict(
        w_in=w_in, norm1_g=w['norm1_g'][l],
        mu=jnp.pad(w['rwkv_mu'][l], (0, RWKV_PAD_W - RWKV_PROJ_W)).reshape(1, RWKV_PAD_W),
        lora=lora.astype(BF16), vecs=vecs,
        qg=jnp.tile(w['q_norm_g'][l], N_HEADS).reshape(1, HW), kg=jnp.tile(w['k_norm_g'][l], N_HEADS).reshape(1, HW),
        ig=pad_lanes(w['idx_k_norm_g'][l]), ib=pad_lanes(w['idx_k_norm_b'][l]),
        ret_gn_g=w['ret_gn_g'][l],
        ln_g=w['rwkv_ln_g'][l].reshape(1, HW), ln_b=w['rwkv_ln_b'][l].reshape(1, HW),
        w_up_a=w['w_up_rwkv'][l].astype(BF16), w_up_b=w['w_up_att'][l].astype(BF16),
        w_up_c=w['w_up_ret'][l].astype(BF16), w_out=w['w_out'][l].astype(BF16),
        norm2_g=w['norm2_g'][l].reshape(1, d),
        router_w=jnp.pad(w['router_w'][l], ((0, 0), (0, LANES - N_EXPERTS))),
        router_b=pad_lanes(w['router_b'][l], NEG_INF),
        moe_w1g=w['moe_w1g'][l], moe_w1l=w['moe_w1l'][l],
        moe_b1g=b1[:, None, 0::2], moe_b1l=b1[:, None, 1::2],
        moe_w2=w['moe_w2'][l].astype(BF16), moe_b2=w['moe_b2'][l][:, None, :],
    )


def _attend_prompt(p3, qn, kn, kiw, rel_bias, tri):
    b, s_len, _ = p3.shape
    qb = 128
    nb = s_len // qb
    n_sel = max(1, min(TOPK_KEYS, s_len // 4))
    scale = HEAD_DIM ** -0.5
    qi = p3[:, :, C_QI:C_QI + HW].reshape(b, nb, qb, N_HEADS, HEAD_DIM).transpose(0, 1, 3, 2, 4)
    qi_stack = (qi * scale).reshape(b, nb, N_HEADS * qb, HEAD_DIM).astype(BF16)
    kiw3 = kiw.reshape(b, s_len, LANES)
    mask = dsa_select(qi_stack, kiw3[:, :, :HEAD_DIM].astype(BF16), kiw3, tri, n_sel)
    r = jnp.arange(qb)[:, None]
    c = jnp.arange(ATT_NEAR)[None, :]
    dist = jnp.stack([r - c, qb + r - c])
    toe = rel_bias[_t5_bucket(dist)].transpose(0, 3, 1, 2)
    toe = toe - rel_bias[REL_BUCKETS - 1][None, :, None, None]
    return dsa_attn((qn.reshape(b, s_len, HW) * scale).astype(BF16), kn.reshape(b, s_len, HW).astype(BF16),
                    p3[:, :, C_V:C_V + HW].astype(BF16), mask, toe)


def _attend_sample(p3, qn, kn, kiw, rel_bias, tri, cache_k_l, cache_v_l, cache_ki_l, page_table):
    db, t_new, _ = p3.shape
    n_pool, page = cache_ki_l.shape[:2]
    n_pages = page_table.shape[1]
    assert page >= REL_MAX_DIST and t_new <= page
    n_sel = max(1, min(TOPK_KEYS, (n_pages * page + t_new) // 4))
    scale = HEAD_DIM ** -0.5
    rows = N_HEADS * t_new
    qi = p3[:, :, C_QI:C_QI + HW].reshape(db, t_new, N_HEADS, HEAD_DIM).transpose(0, 2, 1, 3)
    qi_rows = (qi * scale).reshape(db, rows, HEAD_DIM).astype(BF16)
    kiw3 = kiw.reshape(db, t_new, LANES)
    wcol = kiw3[:, :, HEAD_DIM:HEAD_DIM + N_HEADS].transpose(0, 2, 1).reshape(db, rows, 1)
    wcol = jnp.broadcast_to(wcol, (db, rows, LANES))
    pad_rows = lambda a: jnp.pad(a, ((0, 0), (0, page - t_new), (0, 0)))
    t_of_row = jnp.tile(jnp.arange(t_new), N_HEADS)[:, None]
    j = jnp.arange(page)[None, :]
    dist = jnp.stack([page + t_of_row - j, t_of_row - j])
    h_of_row = jnp.repeat(jnp.arange(N_HEADS), t_new)
    bias_tab = (rel_bias[_t5_bucket(dist), h_of_row[None, :, None]]
                - rel_bias[REL_BUCKETS - 1][h_of_row][None, :, None])
    return dsa_sample(page_table, qi_rows, wcol, qn.reshape(db, t_new, HW) * scale,
                      cache_ki_l, pad_rows(kiw3[:, :, :HEAD_DIM]),
                      cache_k_l.reshape(n_pool, page, HW), pad_rows(kn.reshape(db, t_new, HW)),
                      cache_v_l.reshape(n_pool, page, HW), pad_rows(p3[:, :, C_V:C_V + HW]),
                      bias_tab, tri, n_sel)


def _layer_group(x, mods, lw, consts, rel_bias, pos0, shift0, rwkv0, ret0, attend):
    b, t_len, d = x.shape
    n = b * t_len
    sh1, sc1, gt1, sh2, sc2, gt2 = mods
    x2 = x.reshape(n, d)
    p2 = norm_proj(x2, t_len, lw['norm1_g'], sc1, sh1, lw['w_in'])
    p3 = p2.reshape(b, t_len, PROJ_W)
    prev = jnp.pad(shift0, ((0, 0), (0, RWKV_PAD_W - RWKV_PROJ_W)))[:, None, :]
    r, w, k2, v, kk, kka, g, bv, shift_new = rwkv_prep(p3, prev, lw['mu'], lw['vecs'], lw['lora'],
                                                       consts['seg_ones'])
    tp = -(-t_len // RWKV_CHUNK) * RWKV_CHUNK
    pad_t = lambda a: a if tp == t_len else jnp.pad(a, ((0, 0), (0, tp - t_len), (0, 0)))
    s0 = rwkv0.reshape(b, HW // LANES, 2, HEAD_DIM, HEAD_DIM)
    zero = jnp.zeros_like(s0[:, :, 0])
    s0_bd = jnp.concatenate([jnp.concatenate([s0[:, :, 0], zero], axis=-1),
                             jnp.concatenate([zero, s0[:, :, 1]], axis=-1)], axis=-2)
    ra, s_fin = rwkv_chunked(pad_t(w), pad_t(kk), pad_t(kka), pad_t(k2), pad_t(r), pad_t(v), s0_bd)
    ra = ra[:, :t_len]
    rwkv_new = jnp.stack([s_fin[:, :, :HEAD_DIM, :HEAD_DIM], s_fin[:, :, HEAD_DIM:, HEAD_DIM:]],
                         axis=2).reshape(b, N_HEADS, HEAD_DIM, HEAD_DIM)
    qn, kn, kiw = dsa_prep(p2, lw['qg'], lw['kg'], lw['ig'], lw['ib'], consts['seg_mean'])
    ob = attend(p3, qn, kn, kiw, rel_bias, consts['tri'])
    oc, ret_new = retention(p3, pos0, ret0.reshape(b, HW, RET_DV), lw['ret_gn_g'])
    lw = dict(lw, seg_mean=consts['seg_mean'])
    x1, h2, top_idx, gate = merge(x2, t_len, p2, ra.reshape(n, HW), bv.reshape(n, HW), g.reshape(n, HW),
                                  ob.reshape(n, HW), oc.reshape(n, RET_V_W), gt1, sc2, sh2, lw)
    x_out = moe(x1, t_len, h2, top_idx, gate, gt2, lw)
    return (x_out.reshape(b, t_len, d),
            kn.reshape(b, t_len, N_HEADS, HEAD_DIM),
            p3[:, :, C_V:C_V + HW].reshape(b, t_len, N_HEADS, HEAD_DIM),
            kiw.reshape(b, t_len, LANES)[:, :, :HEAD_DIM],
            rwkv_new,
            shift_new[:, 0, :RWKV_PROJ_W],
            ret_new.reshape(b, N_HEADS, HEAD_DIM, RET_DV))


def kernel(x_prompt, x_sample, c_prompt, c_sample, cache_k, cache_v, cache_kidx, state_rwkv, state_rwkv_shift, state_ret, page_table, rel_bias, ada_w, ada_b, norm1_g, norm2_g, w_in, rwkv_mu, rwkv_w0, rwkv_w2, rwkv_a0, rwkv_a2, rwkv_g2, rwkv_k_k, rwkv_k_a, rwkv_r_k, rwkv_ln_g, rwkv_ln_b, q_norm_g, k_norm_g, idx_k_norm_g, idx_k_norm_b, ret_gn_g, w_up_rwkv, w_up_att, w_up_ret, w_out, router_w, router_b, moe_w1, moe_b1, moe_w2, moe_b2):
    weights = dict(norm1_g=norm1_g, norm2_g=norm2_g, w_in=w_in, rwkv_mu=rwkv_mu, rwkv_w0=rwkv_w0,
                   rwkv_w2=rwkv_w2, rwkv_a0=rwkv_a0, rwkv_a2=rwkv_a2, rwkv_g2=rwkv_g2, rwkv_k_k=rwkv_k_k,
                   rwkv_k_a=rwkv_k_a, rwkv_r_k=rwkv_r_k, rwkv_ln_g=rwkv_ln_g, rwkv_ln_b=rwkv_ln_b,
                   q_norm_g=q_norm_g, k_norm_g=k_norm_g, idx_k_norm_g=idx_k_norm_g,
                   idx_k_norm_b=idx_k_norm_b, ret_gn_g=ret_gn_g, w_up_rwkv=w_up_rwkv, w_up_att=w_up_att,
                   w_up_ret=w_up_ret, w_out=w_out, router_w=router_w, router_b=router_b,
                   moe_b1=moe_b1, moe_w2=moe_w2, moe_b2=moe_b2)
    depth = w_in.shape[0]
    w1_shape = moe_w1.shape
    w1g, w1l = split_even_odd(moe_w1.reshape((-1,) + w1_shape[2:]))
    weights['moe_w1g'] = w1g.reshape(w1_shape[:3] + (w1_shape[3] // 2,))
    weights['moe_w1l'] = w1l.reshape(w1_shape[:3] + (w1_shape[3] // 2,))
    bp, dbatch = x_prompt.shape[0], x_sample.shape[0]
    past = page_table.shape[1] * cache_k.shape[2]
    dt = x_prompt.dtype
    consts = dict(seg_ones=_seg_matrix(1.0), seg_mean=_seg_matrix(1.0 / HEAD_DIM),
                  tri=jnp.asarray(np.triu(np.ones((LANES, LANES), np.float32), 1)).astype(BF16))
    n_c = bp + dbatch
    c_all = jnp.pad(jnp.concatenate([c_prompt, c_sample], axis=0), ((0, -n_c % 8), (0, 0)))
    mods_all = ada_all(c_all, ada_w, ada_b)
    xp, xs = x_prompt, x_sample
    new_p = [[] for _ in range(6)]
    new_s = [[] for _ in range(6)]
    for l in range(depth):
        lw = _pack_layer_weights(l, weights)
        mods_p = jnp.split(mods_all[l, :bp], 6, axis=-1)
        mods_s = jnp.split(mods_all[l, bp:n_c], 6, axis=-1)
        xp, *st_p = _layer_group(xp, mods_p, lw, consts, rel_bias, 0,
                                 jnp.zeros((bp, RWKV_PROJ_W), dt),
                                 jnp.zeros((bp, N_HEADS, HEAD_DIM, HEAD_DIM), dt),
                                 jnp.zeros((bp, N_HEADS, HEAD_DIM, RET_DV), dt),
                                 _attend_prompt)
        attend_s = functools.partial(_attend_sample, cache_k_l=cache_k[l], cache_v_l=cache_v[l],
                                     cache_ki_l=cache_kidx[l], page_table=page_table)
        xs, *st_s = _layer_group(xs, mods_s, lw, consts, rel_bias, past, state_rwkv_shift[l], state_rwkv[l],
                                 state_ret[l], attend_s)
        for lst, a in zip(new_p, st_p):
            lst.append(a)
        for lst, a in zip(new_s, st_s):
            lst.append(a)
    return (xp, xs, *[jnp.stack(a) for a in new_p], *[jnp.stack(a) for a in new_s])
```

```python
import functools
import math

import numpy as np
import jax
import jax.numpy as jnp
from jax import lax
from jax.experimental import pallas as pl
from jax.experimental.pallas import tpu as pltpu

F32 = jnp.float32
BF16 = jnp.bfloat16
I32 = jnp.int32

D_MODEL = 1024
HEAD_DIM = 64
N_HEADS = 8
HW = N_HEADS * HEAD_DIM
LANES = 128
DECAY_LORA, AAA_LORA, GATE_LORA = 64, 64, 160
RWKV_PROJ_W = 3 * HW + DECAY_LORA + AAA_LORA + GATE_LORA
RWKV_PAD_W = 1920
LORA_W = RWKV_PAD_W - 3 * HW
RWKV_GN_EPS = 64e-5
TOPK_KEYS = 256
REL_BUCKETS = 32
REL_MAX_DIST = 128
RET_DV = 2 * HEAD_DIM
RET_V_W = N_HEADS * RET_DV
RET_CHUNK = 128
ROPE_BASE = 10000.0
N_EXPERTS = 32
TOP_K = 4
SWIGLU_LIMIT = 7.0
SWIGLU_ALPHA = 1.702
RMS_EPS = 1e-6
LN_EPS = 1e-5
NEG_INF = -1e30
IN_WIDTHS = (RWKV_PROJ_W, HW, HW, HW, HW, HEAD_DIM, N_HEADS, HW, HW, RET_V_W, RET_V_W, 3 * D_MODEL)

C_RWKV, C_KIWI, C_Q, C_K, C_V, C_QI = 0, 1920, 2048, 2560, 3072, 3584
C_RQ, C_RK, C_RV, C_RG, C_GATE, PROJ_W = 4096, 4608, 5120, 6144, 7168, 10240

VMEM_LIMIT = 56 * 1024 * 1024


def _cparams(*sem):
    return pltpu.CompilerParams(dimension_semantics=sem, vmem_limit_bytes=VMEM_LIMIT)


def _bdot(a, b):
    return jnp.dot(a.astype(BF16), b.astype(BF16), preferred_element_type=F32)


def _bdot_nt(a, b):
    return lax.dot_general(a.astype(BF16), b.astype(BF16), (((1,), (1,)), ((), ())),
                           preferred_element_type=F32)


def _hdot(a, b):
    return jnp.dot(a, b, precision=lax.Precision.HIGHEST, preferred_element_type=F32)


def _split_bf16(a):
    hi = a.astype(BF16)
    return hi, (a - hi.astype(F32)).astype(BF16)


def _mdot(a, b, dims=(((1,), (0,)), ((), ()))):
    ah, al = _split_bf16(a)
    bh, bl = _split_bf16(b)
    d = lambda x, y: lax.dot_general(x, y, dims, preferred_element_type=F32)
    return d(ah, bh) + (d(ah, bl) + d(al, bh))


def _mdot_nt(a, b):
    return _mdot(a, b, (((1,), (1,)), ((), ())))


def _sigmoid(x):
    return 1.0 / (1.0 + jnp.exp(-x))


def _full(shape):
    n = len(shape)
    return pl.BlockSpec(shape, lambda *_: (0,) * n)


def _ada_kernel(c_ref, w_ref, b_ref, o_ref):
    c = c_ref[...]
    o_ref[...] = _bdot(c * _sigmoid(c), w_ref[...]) + b_ref[...]


def ada_all(c_all, ada_w, ada_b):
    n_l, d, d6 = ada_w.shape
    bp = c_all.shape[0]
    tn = 1024
    return pl.pallas_call(
        _ada_kernel,
        grid=(n_l, d6 // tn),
        in_specs=[pl.BlockSpec((bp, d), lambda l, j: (0, 0)),
                  pl.BlockSpec((None, d, tn), lambda l, j: (l, 0, j)),
                  pl.BlockSpec((None, 1, tn), lambda l, j: (l, 0, j))],
        out_specs=pl.BlockSpec((None, bp, tn), lambda l, j: (l, 0, j)),
        out_shape=jax.ShapeDtypeStruct((n_l, bp, d6), F32),
        compiler_params=_cparams("parallel", "parallel"),
        name="ada",
    )(c_all, ada_w, ada_b.reshape(n_l, 1, d6))


def _mod_rows(m, t_len, tm):
    b, d = m.shape
    if t_len % tm == 0:
        per = t_len // tm
        return m[:, None, :], pl.BlockSpec((None, 1, d), lambda i, *_: (i // per, 0, 0))
    assert tm % t_len == 0 and (b * t_len) % tm == 0
    arr = jnp.repeat(m, t_len, axis=0).reshape(b * t_len // tm, tm, d)
    return arr, pl.BlockSpec((None, tm, d), lambda i, *_: (i, 0, 0))


def _norm_proj_kernel(x_ref, g_ref, sc_ref, sh_ref, w_ref, o_ref, h_scr):
    @pl.when(pl.program_id(1) == 0)
    def _():
        x = x_ref[...]
        y = x * lax.rsqrt(jnp.mean(x * x, axis=-1, keepdims=True) + RMS_EPS) * g_ref[...]
        h_scr[...] = (y * (1.0 + sc_ref[...]) + sh_ref[...]).astype(BF16)

    o_ref[...] = jnp.dot(h_scr[...], w_ref[...], preferred_element_type=F32)


def norm_proj(x2, t_len, g, sc, sh, w):
    n, d = x2.shape
    pw = w.shape[1]
    tm = min(n, 1024)
    tn = 1024
    sc3, sc_spec = _mod_rows(sc, t_len, tm)
    sh3, sh_spec = _mod_rows(sh, t_len, tm)
    return pl.pallas_call(
        _norm_proj_kernel,
        grid=(n // tm, pw // tn),
        in_specs=[pl.BlockSpec((tm, d), lambda i, j: (i, 0)),
                  pl.BlockSpec((1, d), lambda i, j: (0, 0)),
                  sc_spec, sh_spec,
                  pl.BlockSpec((d, tn), lambda i, j: (0, j))],
        out_specs=pl.BlockSpec((tm, tn), lambda i, j: (i, j)),
        out_shape=jax.ShapeDtypeStruct((n, pw), F32),
        scratch_shapes=[pltpu.VMEM((tm, d), BF16)],
        compiler_params=_cparams("parallel", "arbitrary"),
        name="norm_proj",
    )(x2, g.reshape(1, d), sc3, sh3, w)


def _rwkv_prep_kernel(p_ref, prev_ref, mu_ref, vec_ref, lw_ref, e_ref,
                      r_o, w_o, k_o, v_o, kk_o, kka_o, g_o, bv_o, shift_o, carry):
    j = pl.program_id(1)
    p = p_ref[...]
    tt = p.shape[0]
    first = jnp.where(j == 0, prev_ref[...], carry[...])
    rows = lax.broadcasted_iota(I32, p.shape, 0)
    p_prev = jnp.where(rows == 0, first, pltpu.roll(p, 1, 0))
    carry[...] = p[tt - 1:tt, :]
    shift_o[...] = p[tt - 1:tt, :]
    xm = p + (p_prev - p) * mu_ref[...]
    r, k, v, xl = xm[:, 0:HW], xm[:, HW:2 * HW], xm[:, 2 * HW:3 * HW], xm[:, 3 * HW:]
    w0, a0, k_k = vec_ref[0:1, :], vec_ref[1:2, :], vec_ref[2:3, :]
    k_a, r_k = vec_ref[3:4, :], vec_ref[4:5, :]
    z = -(w0 + _bdot(jnp.tanh(xl), lw_ref[0]))
    softplus = jnp.maximum(z, 0.0) + jnp.log(1.0 + jnp.exp(-jnp.abs(z)))
    log_decay = -jnp.exp(-softplus - 0.5)
    a = _sigmoid(a0 + _bdot(xl, lw_ref[1]))
    g = _bdot(_sigmoid(xl), lw_ref[2])
    seg_ones = e_ref[...]
    kk = k * k_k
    kk = kk / jnp.maximum(jnp.sqrt(_hdot(kk * kk, seg_ones)), 1e-12)
    k2 = k * (1.0 + (a - 1.0) * k_a)
    r_o[...] = r
    w_o[...] = log_decay
    k_o[...] = k2
    v_o[...] = v
    kk_o[...] = kk
    kka_o[...] = kk * a
    g_o[...] = g
    bv_o[...] = _hdot(r * k2 * r_k, seg_ones) * v


def rwkv_prep(p3, prev, mu, vecs, lora_w, seg_ones):
    b, t_len, _ = p3.shape
    tt = min(t_len, 256)
    tok = pl.BlockSpec((None, tt, HW), lambda i, j: (i, j, 0))
    row = pl.BlockSpec((None, 1, RWKV_PAD_W), lambda i, j: (i, 0, 0))
    tok_shape = jax.ShapeDtypeStruct((b, t_len, HW), F32)
    return pl.pallas_call(
        _rwkv_prep_kernel,
        grid=(b, t_len // tt),
        in_specs=[pl.BlockSpec((None, tt, RWKV_PAD_W), lambda i, j: (i, j, 0)), row,
                  _full((1, RWKV_PAD_W)), _full((8, HW)), _full((3, LORA_W, HW)), _full((HW, HW))],
        out_specs=[tok] * 8 + [row],
        out_shape=[tok_shape] * 8 + [jax.ShapeDtypeStruct((b, 1, RWKV_PAD_W), F32)],
        scratch_shapes=[pltpu.VMEM((1, RWKV_PAD_W), F32)],
        compiler_params=_cparams("parallel", "arbitrary"),
        name="rwkv_prep",
    )(p3, prev, mu, vecs, lora_w, seg_ones)


RWKV_CHUNK = 64
SOLVE_BLOCK = 16


def _rwkv_chunk_kernel(lw_ref, kk_ref, kka_ref, k_ref, r_ref, v_ref, s0_ref, tril_ref, o_ref, sfin_ref, st):
    c = pl.program_id(1)

    @pl.when(c == 0)
    def _():
        st[...] = s0_ref[...]

    n = RWKV_CHUNK
    lw = lw_ref[...]
    lc = _hdot(tril_ref[...], lw)
    lc_last = lc[n - 1:n, :]
    p_inv = jnp.exp(-lc)
    p_rem = jnp.exp(lc_last - lc)
    a_hat = -kk_ref[...] * jnp.exp(lc - lw)
    r_hat = r_ref[...] * jnp.exp(lc)
    b_hat = kka_ref[...] * p_inv
    k_hat = k_ref[...] * p_inv
    b_til = kka_ref[...] * p_rem
    k_til = k_ref[...] * p_rem
    v = v_ref[...]
    p_last = jnp.exp(lc_last)
    row = lax.broadcasted_iota(I32, (2 * n, 2 * n), 0)
    col = lax.broadcasted_iota(I32, (2 * n, 2 * n), 1)
    strict = (col % n) < (row % n)
    incl = (col % n) <= (row % n)
    same_block = (row // SOLVE_BLOCK) == (col // SOLVE_BLOCK)
    same_head = (row // n) == (col // n)
    eye = jnp.where(row == col, 1.0, 0.0)
    low = lax.broadcasted_iota(I32, (n, LANES), 1) < HEAD_DIM
    for p in range(HW // LANES):
        lanes = slice(p * LANES, (p + 1) * LANES)

        def stacked(x):
            xp = x[:, lanes]
            return jnp.concatenate([jnp.where(low, xp, 0.0), jnp.where(low, 0.0, xp)], axis=0)

        ar = jnp.concatenate([stacked(a_hat), stacked(r_hat)], axis=0)
        vs = stacked(v)
        gb = _bdot_nt(ar, stacked(b_hat))
        gk = _bdot_nt(ar, stacked(k_hat))
        l_ab = jnp.where(strict, gb[0:2 * n], 0.0)
        l_ak = jnp.where(strict, gk[0:2 * n], 0.0)
        n_rb = jnp.where(incl, gb[2 * n:4 * n], 0.0)
        n_rk = jnp.where(incl, gk[2 * n:4 * n], 0.0)
        state = st[p]
        t0 = _bdot_nt(ar, state)
        rhs = t0[0:2 * n] + _bdot(l_ak, vs)
        l_d = jnp.where(same_block, l_ab, 0.0)
        l_o = l_ab - l_d
        l_2 = _mdot(l_d, l_d)
        q = eye + l_d
        q = q + _mdot(q, l_2)
        l_4 = _mdot(l_2, l_2)
        q = q + _mdot(q, l_4)
        q = q + _mdot(q, _mdot(l_4, l_4))
        w_o = _mdot(q, l_o)
        z = _mdot(q, rhs)
        y = z + _mdot(_mdot(w_o, w_o), z)
        sa = y + _mdot(w_o, y)
        out = t0[2 * n:4 * n] + _bdot(jnp.concatenate([n_rb, n_rk], axis=1), jnp.concatenate([sa, vs], axis=0))
        o_ref[:, lanes] = out[0:n] + out[n:2 * n]
        x = jnp.concatenate([sa[0:n] + sa[n:2 * n], v[:, lanes]], axis=0)
        upd = _bdot(x.T, jnp.concatenate([b_til[:, lanes], k_til[:, lanes]], axis=0))
        st[p] = state * p_last[:, lanes] + jnp.where(same_head, upd, 0.0)
    sfin_ref[...] = st[...]


def rwkv_chunked(lw, kk, kka, k2, r, v, s0_bd):
    b, tp, _ = lw.shape
    n_pair = HW // LANES
    tok = pl.BlockSpec((None, RWKV_CHUNK, HW), lambda i, j: (i, j, 0))
    state = pl.BlockSpec((None, n_pair, LANES, LANES), lambda i, j: (i, 0, 0, 0))
    tril = jnp.asarray(np.tril(np.ones((RWKV_CHUNK, RWKV_CHUNK), np.float32)))
    return pl.pallas_call(
        _rwkv_chunk_kernel,
        grid=(b, tp // RWKV_CHUNK),
        in_specs=[tok] * 6 + [state, _full((RWKV_CHUNK, RWKV_CHUNK))],
        out_specs=[tok, state],
        out_shape=[jax.ShapeDtypeStruct((b, tp, HW), F32), jax.ShapeDtypeStruct((b, n_pair, LANES, LANES), F32)],
        scratch_shapes=[pltpu.VMEM((n_pair, LANES, LANES), F32)],
        compiler_params=_cparams("parallel", "arbitrary"),
        name="rwkv_chunked",
    )(lw, kk, kka, k2, r, v, s0_bd, tril)


def _dsa_prep_kernel(kiwi_ref, qk_ref, qg_ref, kg_ref, ig_ref, ib_ref, e_ref, qn_o, kn_o, kiw_o):
    seg_mean = e_ref[...]
    q, k = qk_ref[:, 0:HW], qk_ref[:, HW:2 * HW]
    qn_o[...] = q * lax.rsqrt(_hdot(q * q, seg_mean) + RMS_EPS) * qg_ref[...]
    kn_o[...] = k * lax.rsqrt(_hdot(k * k, seg_mean) + RMS_EPS) * kg_ref[...]
    x = kiwi_ref[...]
    is_key = lax.broadcasted_iota(I32, x.shape, 1) < HEAD_DIM
    mu = jnp.sum(jnp.where(is_key, x, 0.0), axis=-1, keepdims=True) / HEAD_DIM
    dev = jnp.where(is_key, x - mu, 0.0)
    var = jnp.sum(dev * dev, axis=-1, keepdims=True) / HEAD_DIM
    kin = dev * lax.rsqrt(var + LN_EPS) * ig_ref[...] + ib_ref[...]
    kiw_o[...] = jnp.where(is_key, kin, x * N_HEADS ** -0.5)


def dsa_prep(p2, qg, kg, ig, ib, seg_mean):
    n = p2.shape[0]
    tm = min(n, 512)
    return pl.pallas_call(
        _dsa_prep_kernel,
        grid=(n // tm,),
        in_specs=[pl.BlockSpec((tm, LANES), lambda i: (i, C_KIWI // LANES)),
                  pl.BlockSpec((tm, 2 * HW), lambda i: (i, C_Q // (2 * HW))),
                  _full((1, HW)), _full((1, HW)), _full((1, LANES)), _full((1, LANES)), _full((HW, HW))],
        out_specs=[pl.BlockSpec((tm, HW), lambda i: (i, 0)), pl.BlockSpec((tm, HW), lambda i: (i, 0)),
                   pl.BlockSpec((tm, LANES), lambda i: (i, 0))],
        out_shape=[jax.ShapeDtypeStruct((n, HW), F32), jax.ShapeDtypeStruct((n, HW), F32),
                   jax.ShapeDtypeStruct((n, LANES), F32)],
        compiler_params=_cparams("parallel"),
        name="dsa_prep",
    )(p2, p2, qg, kg, ig, ib, seg_mean)


INT_MIN = -2 ** 31
SEL_CHUNK = 512


def _order_key(score):
    bits = lax.bitcast_convert_type(score + 0.0, I32)
    return bits ^ ((bits >> 31) & 0x7FFFFFFF)


def _count_ge(key_ref, n_chunks, cand, rows, strict=False):
    def body(c, acc):
        blk = key_ref[:, pl.ds(pl.multiple_of(c * SEL_CHUNK, SEL_CHUNK), SEL_CHUNK)]
        hit = (blk > cand) if strict else (blk >= cand)
        ones = jnp.where(hit, 1.0, 0.0)
        for g in range(SEL_CHUNK // LANES):
            acc = acc + ones[:, g * LANES:(g + 1) * LANES]
        return acc
    acc = lax.fori_loop(0, n_chunks, body, jnp.zeros((rows, LANES), F32))
    return jnp.sum(acc, axis=1, keepdims=True)


def _kth_largest_key(key_ref, n_chunks, n_sel, rows):
    cnt0 = _count_ge(key_ref, n_chunks, jnp.zeros((rows, 1), I32), rows)
    thr = jnp.where(cnt0 >= n_sel, 0, INT_MIN).astype(I32)
    n_keys = jnp.full((rows, 1), n_chunks * SEL_CHUNK, I32).astype(F32)
    cnt_thr = jnp.where(cnt0 >= n_sel, cnt0, n_keys)

    def bit_body(it, carry):
        def refine(carry):
            thr, cnt_thr = carry
            cand = thr + jnp.left_shift(jnp.int32(1), 30 - it)
            cnt = _count_ge(key_ref, n_chunks, cand, rows)
            keep = cnt >= n_sel
            return jnp.where(keep, cand, thr), jnp.where(keep, cnt, cnt_thr)

        settled = jnp.max(carry[1]) <= n_sel
        return lax.cond(settled, lambda c: c, refine, carry)

    return lax.fori_loop(0, 31, bit_body, (thr, cnt_thr))[0]


def _select_into(key_ref, valid_fn, n_chunks, n_sel, rows, tri_ref, write_fn):
    thr = _kth_largest_key(key_ref, n_chunks, n_sel, rows)
    n_gt = _count_ge(key_ref, n_chunks, thr, rows, strict=True)
    need = n_sel - n_gt

    def chunk(c):
        return key_ref[:, pl.ds(pl.multiple_of(c * SEL_CHUNK, SEL_CHUNK), SEL_CHUNK)]

    def count_ties(c, acc):
        return acc + jnp.sum(jnp.where((chunk(c) == thr) & valid_fn(c), 1.0, 0.0), axis=1, keepdims=True)

    n_eq = lax.fori_loop(0, n_chunks, count_ties, jnp.zeros((rows, 1), F32))
    surplus = jnp.max(n_eq - need) > 0.0

    @pl.when(jnp.logical_not(surplus))
    def _():
        def body(c, carry):
            blk = chunk(c)
            write_fn(c, (blk >= thr) & valid_fn(c))
            return carry
        lax.fori_loop(0, n_chunks, body, 0)

    @pl.when(surplus)
    def _():
        def body(c, seen):
            blk = chunk(c)
            val = valid_fn(c)
            tie = jnp.where((blk == thr) & val, 1.0, 0.0)
            parts = []
            for g in range(SEL_CHUNK // LANES):
                tg = tie[:, g * LANES:(g + 1) * LANES]
                before = _bdot(tg, tri_ref[...]) + seen
                parts.append(jnp.where(before < need, tg, 0.0))
                seen = seen + jnp.sum(tg, axis=1, keepdims=True)
            take = jnp.concatenate(parts, axis=1) > 0.0
            write_fn(c, ((blk > thr) & val) | take)
            return seen
        lax.fori_loop(0, n_chunks, body, jnp.zeros((rows, 1), F32))


def _dsa_select_kernel(qi_ref, ki_ref, wi_ref, tri_ref, m_ref, key_scr, *, n_sel, qb):
    i = pl.program_id(1)
    n_chunks = (i * qb + qb + SEL_CHUNK - 1) // SEL_CHUNK
    m_ref[...] = jnp.zeros(m_ref.shape, m_ref.dtype)
    qpos = i * qb + lax.broadcasted_iota(I32, (qb, SEL_CHUNK), 0)
    lane_pos = lax.broadcasted_iota(I32, (qb, SEL_CHUNK), 1)
    wi = wi_ref[...]

    def causal(c):
        return lane_pos + c * SEL_CHUNK <= qpos

    def score_chunk(c, carry):
        off = pl.multiple_of(c * SEL_CHUNK, SEL_CHUNK)
        s = jnp.maximum(_bdot_nt(qi_ref[...], ki_ref[pl.ds(off, SEL_CHUNK), :]), 0.0)
        acc = jnp.zeros((qb, SEL_CHUNK), F32)
        for h in range(N_HEADS):
            acc = acc + s[h * qb:(h + 1) * qb, :] * wi[:, HEAD_DIM + h:HEAD_DIM + h + 1]
        key_scr[:, pl.ds(off, SEL_CHUNK)] = _order_key(jnp.where(causal(c), acc, NEG_INF))
        return carry

    lax.fori_loop(0, n_chunks, score_chunk, 0)

    def write(c, sel):
        off = pl.multiple_of(c * SEL_CHUNK, SEL_CHUNK)
        m_ref[:, pl.ds(off, SEL_CHUNK)] = jnp.where(sel, 1, 0).astype(m_ref.dtype)

    _select_into(key_scr, causal, n_chunks, n_sel, qb, tri_ref, write)


def dsa_select(qi_stack, ki, kiw3, tri, n_sel):
    b, nb, rows, _ = qi_stack.shape
    qb = rows // N_HEADS
    s_len = ki.shape[1]
    return pl.pallas_call(
        functools.partial(_dsa_select_kernel, n_sel=n_sel, qb=qb),
        grid=(b, nb),
        in_specs=[pl.BlockSpec((None, None, rows, HEAD_DIM), lambda bi, i: (bi, i, 0, 0)),
                  pl.BlockSpec((None, s_len, HEAD_DIM), lambda bi, i: (bi, 0, 0)),
                  pl.BlockSpec((None, qb, LANES), lambda bi, i: (bi, i, 0)),
                  _full((LANES, LANES))],
        out_specs=pl.BlockSpec((None, qb, s_len), lambda bi, i: (bi, i, 0)),
        out_shape=jax.ShapeDtypeStruct((b, s_len, s_len), jnp.int8),
        scratch_shapes=[pltpu.VMEM((qb, s_len), I32)],
        compiler_params=_cparams("parallel", "arbitrary"),
        name="dsa_select",
    )(qi_stack, ki, kiw3, tri)


ATT_CHUNK = 512
ATT_NEAR = 256
M_INIT = -1e29


def _softmax_step(carry, s, v, v_transposed=False):
    m, l, acc = carry
    m_new = jnp.maximum(m, jnp.max(s, axis=1, keepdims=True))
    alpha = jnp.exp(m - m_new)
    p = jnp.exp(s - m_new)
    pv = _bdot_nt(p, v) if v_transposed else _bdot(p, v)
    return m_new, l * alpha + jnp.sum(p, axis=1, keepdims=True), acc * alpha + pv


def _dsa_attn_kernel(q_ref, k_ref, v_ref, m_ref, toe_ref, o_ref, *, qb):
    i = pl.program_id(1)
    near_start = jnp.maximum(i - 1, 0) * qb
    n_far = (near_start + ATT_CHUNK - 1) // ATT_CHUNK
    n_pair = HW // LANES
    low = lax.broadcasted_iota(I32, (qb, LANES), 1) < HEAD_DIM
    kpos = lax.broadcasted_iota(I32, (qb, ATT_CHUNK), 1)
    q = q_ref[...]
    q2 = []
    for p in range(n_pair):
        qp = q[:, p * LANES:(p + 1) * LANES]
        zero = jnp.zeros_like(qp)
        q2.append(jnp.concatenate([jnp.where(low, qp, zero), jnp.where(low, zero, qp)], axis=0))

    def far_chunk(c, carry):
        off = pl.multiple_of(c * ATT_CHUNK, ATT_CHUNK)
        sel = m_ref[:, pl.ds(off, ATT_CHUNK)].astype(I32) != 0
        add = jnp.where(sel & (kpos + off < near_start), 0.0, NEG_INF)
        add2 = jnp.concatenate([add, add], axis=0)
        out = []
        for p in range(n_pair):
            lanes = slice(p * LANES, (p + 1) * LANES)
            s = _bdot_nt(q2[p], k_ref[pl.ds(off, ATT_CHUNK), lanes]) + add2
            out.append(_softmax_step(carry[p], s, v_ref[pl.ds(off, ATT_CHUNK), lanes]))
        return tuple(out)

    init = tuple((jnp.full((2 * qb, 1), M_INIT, F32), jnp.zeros((2 * qb, 1), F32),
                  jnp.zeros((2 * qb, LANES), F32)) for _ in range(n_pair))
    carry = lax.fori_loop(0, n_far, far_chunk, init)
    off = pl.multiple_of(near_start, qb)
    add = jnp.where(m_ref[:, pl.ds(off, ATT_NEAR)].astype(I32) != 0, 0.0, NEG_INF)
    for p in range(n_pair):
        lanes = slice(p * LANES, (p + 1) * LANES)
        bias2 = jnp.concatenate([toe_ref[2 * p] + add, toe_ref[2 * p + 1] + add], axis=0)
        s = _bdot_nt(q2[p], k_ref[pl.ds(off, ATT_NEAR), lanes]) + bias2
        _, l, acc = _softmax_step(carry[p], s, v_ref[pl.ds(off, ATT_NEAR), lanes])
        o = acc / l
        o_ref[:, lanes] = jnp.where(low, o[0:qb, :], o[qb:2 * qb, :])


def dsa_attn(q, k, v, mask, toe):
    b, s_len, _ = q.shape
    qb = 128
    kv = pl.BlockSpec((None, s_len, HW), lambda bi, i: (bi, 0, 0))
    return pl.pallas_call(
        functools.partial(_dsa_attn_kernel, qb=qb),
        grid=(b, s_len // qb),
        in_specs=[pl.BlockSpec((None, qb, HW), lambda bi, i: (bi, i, 0)), kv, kv,
                  pl.BlockSpec((None, qb, s_len), lambda bi, i: (bi, i, 0)),
                  pl.BlockSpec((None, N_HEADS, qb, ATT_NEAR), lambda bi, i: (jnp.minimum(i, 1), 0, 0, 0))],
        out_specs=pl.BlockSpec((None, qb, HW), lambda bi, i: (bi, i, 0)),
        out_shape=jax.ShapeDtypeStruct((b, s_len, HW), F32),
        compiler_params=_cparams("parallel", "arbitrary"),
        name="dsa_attn",
    )(q, k, v, mask, toe)


def _dsa_sample_kernel(pt_ref, qi_ref, wcol_ref, q_ref, *refs, n_pages, n_sel, t_new, page, group):
    cki, nki_ref = refs[0:group], refs[group]
    ck, nk_ref = refs[group + 1:2 * group + 1], refs[2 * group + 1]
    cv, nv_ref = refs[2 * group + 2:3 * group + 2], refs[3 * group + 2]
    bias_ref, tri_ref, o_ref, key_scr, sel_scr, m_scr, l_scr, acc_scr = refs[3 * group + 3:]
    phase = pl.program_id(1)
    g = pl.program_id(2)
    rows = N_HEADS * t_new
    gw = group * page
    new_off = n_pages * page
    n_chunks = -(-(n_pages + 1) * page // SEL_CHUNK)
    last_group = g == n_pages // group - 1

    def index_scores(ki_t):
        s = jnp.maximum(_bdot(qi_ref[...], ki_t), 0.0) * wcol_ref[:, 0:1]
        acc = jnp.zeros((t_new, s.shape[1]), F32)
        for h in range(N_HEADS):
            acc = acc + s[h * t_new:(h + 1) * t_new, :]
        return acc

    @pl.when(jnp.logical_and(phase == 0, g == 0))
    def _():
        key_scr[...] = jnp.full(key_scr.shape, INT_MIN, I32)

    @pl.when(phase == 0)
    def _():
        ki_t = jnp.concatenate([r[...].astype(BF16) for r in cki], axis=1)
        key_scr[:, pl.ds(pl.multiple_of(g * gw, gw), gw)] = _order_key(index_scores(ki_t))

        @pl.when(last_group)
        def _():
            ok = lax.broadcasted_iota(I32, (t_new, page), 1) <= lax.broadcasted_iota(I32, (t_new, page), 0)
            key_scr[:, new_off:new_off + page] = _order_key(jnp.where(ok, index_scores(nki_ref[...]), NEG_INF))

    @pl.when(jnp.logical_and(phase == 1, g == 0))
    def _():
        pos = lax.broadcasted_iota(I32, (t_new, SEL_CHUNK), 1)
        qpos = new_off + lax.broadcasted_iota(I32, (t_new, SEL_CHUNK), 0)

        def valid(c):
            return pos + c * SEL_CHUNK <= qpos

        def write(c, sel):
            sel_scr[:, pl.ds(pl.multiple_of(c * SEL_CHUNK, SEL_CHUNK), SEL_CHUNK)] = jnp.where(sel, 0.0, NEG_INF)

        _select_into(key_scr, valid, n_chunks, n_sel, t_new, tri_ref, write)
        m_scr[...] = jnp.full(m_scr.shape, M_INIT, F32)
        l_scr[...] = jnp.zeros(l_scr.shape, F32)
        acc_scr[...] = jnp.zeros(acc_scr.shape, F32)

    @pl.when(phase == 1)
    def _():
        head_of_row = lax.broadcasted_iota(I32, (rows, HW), 0) // t_new
        own_head = head_of_row == lax.broadcasted_iota(I32, (rows, HW), 1) // HEAD_DIM
        qx = jnp.where(own_head, jnp.concatenate([q_ref[...]] * N_HEADS, axis=0), 0.0)

        def attend(k_t, v_t, add_rows, bias):
            s = _bdot(qx, k_t) + jnp.concatenate([add_rows] * N_HEADS, axis=0) + bias
            m, l, acc = _softmax_step((m_scr[...], l_scr[...], acc_scr[...]), s, v_t, v_transposed=True)
            m_scr[...] = m
            l_scr[...] = l
            acc_scr[...] = acc

        k_t = jnp.concatenate([r[...].astype(BF16) for r in ck], axis=1)
        v_t = jnp.concatenate([r[...].astype(BF16) for r in cv], axis=1)
        near = jnp.where(last_group, bias_ref[0], 0.0)
        bias = jnp.concatenate([jnp.zeros((rows, gw - page), F32), near], axis=1) if gw > page else near
        attend(k_t, v_t, sel_scr[:, pl.ds(pl.multiple_of(g * gw, gw), gw)], bias)

        @pl.when(last_group)
        def _():
            attend(nk_ref[...], nv_ref[...], sel_scr[:, new_off:new_off + page], bias_ref[1])
            o_full = jnp.where(own_head, acc_scr[...] / l_scr[...], 0.0)
            out = jnp.zeros((t_new, HW), F32)
            for h in range(N_HEADS):
                out = out + o_full[h * t_new:(h + 1) * t_new, :]
            o_ref[...] = out


def dsa_sample(page_table, layer, qi_rows, wcol, q3, cache_ki, new_ki, cache_k, new_k, cache_v, new_v, bias_tab, tri,
               n_sel):
    db, n_pages = page_table.shape
    page = cache_ki.shape[3]
    t_new = q3.shape[1]
    rows = N_HEADS * t_new
    group = math.gcd(n_pages, 8)
    n_groups = n_pages // group
    n_chunks = -(-(n_pages + 1) * page // SEL_CHUNK)

    def cache_spec(width, active_phase, idle_group, u):
        def index(b, ph, g, pt):
            use = jnp.where(ph == active_phase, g, idle_group) * group + u
            return (layer, pt[b * n_pages + use], 0, 0)
        return pl.BlockSpec((None, None, width, page), index)

    def per_b(shape):
        return pl.BlockSpec((None,) + shape, lambda b, ph, g, pt: (b, 0, 0))

    ki_specs = [cache_spec(HEAD_DIM, 0, n_groups - 1, u) for u in range(group)]
    kv_specs = [cache_spec(HW, 1, 0, u) for u in range(group)]
    grid_spec = pltpu.PrefetchScalarGridSpec(
        num_scalar_prefetch=1,
        grid=(db, 2, n_groups),
        in_specs=[per_b((rows, HEAD_DIM)), per_b((rows, LANES)), per_b((t_new, HW))]
                 + ki_specs + [per_b((HEAD_DIM, page))] + kv_specs + [per_b((HW, page))]
                 + kv_specs + [per_b((HW, page))]
                 + [pl.BlockSpec((2, rows, page), lambda b, ph, g, pt: (0, 0, 0)),
                    pl.BlockSpec((LANES, LANES), lambda b, ph, g, pt: (0, 0))],
        out_specs=pl.BlockSpec((None, t_new, HW), lambda b, ph, g, pt: (b, 0, 0)),
        scratch_shapes=[pltpu.VMEM((t_new, n_chunks * SEL_CHUNK), I32),
                        pltpu.VMEM((t_new, n_chunks * SEL_CHUNK), F32),
                        pltpu.VMEM((rows, 1), F32), pltpu.VMEM((rows, 1), F32), pltpu.VMEM((rows, HW), F32)])
    return pl.pallas_call(
        functools.partial(_dsa_sample_kernel, n_pages=n_pages, n_sel=n_sel, t_new=t_new, page=page, group=group),
        grid_spec=grid_spec,
        out_shape=jax.ShapeDtypeStruct((db, t_new, HW), F32),
        compiler_params=_cparams("parallel", "arbitrary", "arbitrary"),
        name="dsa_sample",
    )(page_table.reshape(-1), qi_rows, wcol, q3, *([cache_ki] * group), new_ki, *([cache_k] * group), new_k,
      *([cache_v] * group), new_v, bias_tab, tri)


def _retention_kernel(q_ref, k_ref, v_ref, g_ref, cos_ref, sin_ref, dmask_ref, qdec_ref, kdec_ref, cdec_ref,
                      gn_ref, s0_ref, o_ref, sfin_ref, st, *, rows):
    c = pl.program_id(1)

    @pl.when(c == 0)
    def _():
        st[...] = s0_ref[...]

    def padded(x):
        if x.shape[0] == RET_CHUNK:
            return x
        return jnp.concatenate([x, jnp.zeros((RET_CHUNK - x.shape[0], x.shape[1]), x.dtype)], axis=0)

    lane = lax.broadcasted_iota(I32, (RET_CHUNK, HW), 1)
    first_half = (lane % HEAD_DIM) < HEAD_DIM // 2

    def rope(x):
        partner = jnp.where(first_half, pltpu.roll(x, HW - HEAD_DIM // 2, 1), pltpu.roll(x, HEAD_DIM // 2, 1))
        return x * padded(cos_ref[...]) + partner * padded(sin_ref[...])

    q = rope(padded(q_ref[...]))
    k = rope(padded(k_ref[...])) * HEAD_DIM ** -0.5
    v = padded(v_ref[...])
    state = st[...]
    head_of_lane = lane // HEAD_DIM
    qdec = qdec_ref[...]
    o_parts = []
    for h in range(N_HEADS):
        qh = jnp.where(head_of_lane == h, q, 0.0)
        inner = _bdot_nt(qh, k) * dmask_ref[h]
        vh = v[:, h * RET_DV:(h + 1) * RET_DV]
        oh = _bdot(inner, vh) + _bdot(qh, state) * qdec[:, h:h + 1]
        mu = jnp.mean(oh, axis=-1, keepdims=True)
        dev = oh - mu
        var = jnp.mean(dev * dev, axis=-1, keepdims=True)
        o_parts.append(dev * lax.rsqrt(var + LN_EPS))
    o = jnp.concatenate(o_parts, axis=1) * gn_ref[...]
    gate = g_ref[...]
    o_ref[...] = gate * _sigmoid(gate) * o[0:rows, :]
    kv = _bdot((k * kdec_ref[...]).T, v)
    upd = jnp.concatenate([kv[h * HEAD_DIM:(h + 1) * HEAD_DIM, h * RET_DV:(h + 1) * RET_DV]
                           for h in range(N_HEADS)], axis=0)
    st[...] = state * cdec_ref[...] + upd
    sfin_ref[...] = st[...]


def retention(p3, pos0, s0, gn_g):
    b, t_len, _ = p3.shape
    chunk = math.gcd(t_len, RET_CHUNK)
    n_chunk = t_len // chunk
    half = HEAD_DIM // 2
    inv = ROPE_BASE ** (-jnp.arange(half, dtype=F32) / half)
    ang = (pos0 + jnp.arange(t_len)).astype(F32)[:, None] * inv[None, :]
    cos = jnp.tile(jnp.cos(ang), (1, 2 * N_HEADS))
    sin = jnp.tile(jnp.concatenate([-jnp.sin(ang), jnp.sin(ang)], axis=1), (1, N_HEADS))
    log_g = jnp.log(1.0 - 2.0 ** (-5.0 - jnp.arange(N_HEADS, dtype=F32)))
    idx = jnp.arange(RET_CHUNK, dtype=F32)
    rel = idx[:, None] - idx[None, :]
    dmask = jnp.where(rel[None] >= 0, jnp.exp(rel[None] * log_g[:, None, None]), 0.0)
    qdec = jnp.pad(jnp.exp((idx + 1.0)[:, None] * log_g[None, :]), ((0, 0), (0, LANES - N_HEADS)))
    kdec = jnp.where((idx < chunk)[:, None], jnp.exp((chunk - 1.0 - idx)[:, None] * log_g[None, :]), 0.0)
    kdec = jnp.repeat(kdec, HEAD_DIM, axis=1)
    cdec = jnp.broadcast_to(jnp.repeat(jnp.exp(chunk * log_g), HEAD_DIM)[:, None], (HW, RET_DV))
    tok = lambda width, col: pl.BlockSpec((None, chunk, width), lambda i, j: (i, j, col))
    state = pl.BlockSpec((None, HW, RET_DV), lambda i, j: (i, 0, 0))
    return pl.pallas_call(
        functools.partial(_retention_kernel, rows=chunk),
        grid=(b, n_chunk),
        in_specs=[tok(HW, C_RQ // HW), tok(HW, C_RK // HW), tok(RET_V_W, C_RV // RET_V_W),
                  tok(RET_V_W, C_RG // RET_V_W),
                  pl.BlockSpec((chunk, HW), lambda i, j: (j, 0)), pl.BlockSpec((chunk, HW), lambda i, j: (j, 0)),
                  _full((N_HEADS, RET_CHUNK, RET_CHUNK)), _full((RET_CHUNK, LANES)), _full((RET_CHUNK, HW)),
                  _full((HW, RET_DV)), _full((1, RET_V_W)), state],
        out_specs=[pl.BlockSpec((None, chunk, RET_V_W), lambda i, j: (i, j, 0)), state],
        out_shape=[jax.ShapeDtypeStruct((b, t_len, RET_V_W), F32), jax.ShapeDtypeStruct((b, HW, RET_DV), F32)],
        scratch_shapes=[pltpu.VMEM((HW, RET_DV), F32)],
        compiler_params=_cparams("parallel", "arbitrary"),
        name="retention",
    )(p3, p3, p3, p3, cos, sin, dmask, qdec, kdec, cdec, gn_g.reshape(1, RET_V_W), s0)


def _merge_kernel(x_ref, ra_ref, bv_ref, g_ref, ob_ref, oc_ref, ga_ref, gb_ref, gc_ref,
                  gt1_ref, sc2_ref, sh2_ref, lng_ref, lnb_ref, e_ref, wa_ref, wb_ref, wc_ref, wo_ref,
                  n2_ref, rw_ref, rb_ref, x1_o, h2_o, idx_o, gate_o):
    seg_mean = e_ref[...]
    ra = ra_ref[...]
    mu = _hdot(ra, seg_mean)
    dev = ra - mu
    var = _hdot(dev * dev, seg_mean)
    oa = (dev * lax.rsqrt(var + RWKV_GN_EPS) * lng_ref[...] + lnb_ref[...] + bv_ref[...]) * g_ref[...]
    merged = (_sigmoid(ga_ref[...]) * _bdot(oa, wa_ref[...])
              + _sigmoid(gb_ref[...]) * _bdot(ob_ref[...], wb_ref[...])
              + _sigmoid(gc_ref[...]) * _bdot(oc_ref[...], wc_ref[...]))
    x1 = x_ref[...] + gt1_ref[...] * _bdot(merged, wo_ref[...])
    x1_o[...] = x1
    y = x1 * lax.rsqrt(jnp.mean(x1 * x1, axis=-1, keepdims=True) + RMS_EPS) * n2_ref[...]
    h2 = y * (1.0 + sc2_ref[...]) + sh2_ref[...]
    h2_o[...] = h2
    logits = _hdot(h2, rw_ref[...]) + rb_ref[...]
    lane = lax.broadcasted_iota(I32, logits.shape, 1).astype(F32)
    idx_acc = jnp.zeros(logits.shape, F32)
    val_acc = jnp.zeros(logits.shape, F32)
    top = None
    denom = jnp.zeros((logits.shape[0], 1), F32)
    for k in range(TOP_K):
        m = jnp.max(logits, axis=-1, keepdims=True)
        ix = jnp.min(jnp.where(logits == m, lane, float(LANES)), axis=-1, keepdims=True)
        top = m if top is None else top
        ev = jnp.exp(m - top)
        denom = denom + ev
        idx_acc = jnp.where(lane == k, ix, idx_acc)
        val_acc = jnp.where(lane == k, ev, val_acc)
        logits = jnp.where(lane == ix, -jnp.inf, logits)
    idx_o[...] = idx_acc.astype(I32)
    gate_o[...] = val_acc / denom


def merge(x2, t_len, p2, ra, bv, g, ob, oc, gt1, sc2, sh2, lw):
    n, d = x2.shape
    tm = min(n, 256)
    mods = [_mod_rows(m, t_len, tm) for m in (gt1, sc2, sh2)]
    row = lambda width, col=0: pl.BlockSpec((tm, width), lambda i: (i, col))
    gate_col = C_GATE // d
    out_shape = [jax.ShapeDtypeStruct((n, d), F32), jax.ShapeDtypeStruct((n, d), F32),
                 jax.ShapeDtypeStruct((n, LANES), I32), jax.ShapeDtypeStruct((n, LANES), F32)]
    return pl.pallas_call(
        _merge_kernel,
        grid=(n // tm,),
        in_specs=[row(d), row(HW), row(HW), row(HW), row(HW), row(RET_V_W),
                  row(d, gate_col), row(d, gate_col + 1), row(d, gate_col + 2)]
                 + [m[1] for m in mods]
                 + [_full((1, HW)), _full((1, HW)), _full((HW, HW)), _full((HW, d)), _full((HW, d)),
                    _full((RET_V_W, d)), _full((d, d)), _full((1, d)), _full((d, LANES)), _full((1, LANES))],
        out_specs=[row(d), row(d), row(LANES), row(LANES)],
        out_shape=out_shape,
        compiler_params=_cparams("parallel"),
        name="merge",
    )(x2, ra, bv, g, ob, oc, p2, p2, p2, *[m[0] for m in mods],
      lw['ln_g'], lw['ln_b'], lw['seg_mean'], lw['w_up_a'], lw['w_up_b'], lw['w_up_c'], lw['w_out'],
      lw['norm2_g'], lw['router_w'], lw['router_b'])


SPLIT_W = 2 * LANES


def _split_even_odd_kernel(w_ref, perm_ref, even_o, odd_o):
    perm = perm_ref[...]
    for c in range(w_ref.shape[1] // SPLIT_W):
        blk = jnp.dot(w_ref[:, c * SPLIT_W:(c + 1) * SPLIT_W].astype(BF16), perm, preferred_element_type=F32)
        even_o[:, c * LANES:(c + 1) * LANES] = blk[:, 0:LANES].astype(BF16)
        odd_o[:, c * LANES:(c + 1) * LANES] = blk[:, LANES:SPLIT_W].astype(BF16)


def split_even_odd(w):
    n_e, d, f2 = w.shape
    tk = 512
    m = np.arange(LANES)
    perm = np.zeros((SPLIT_W, SPLIT_W), np.float32)
    perm[2 * m, m] = 1.0
    perm[2 * m + 1, LANES + m] = 1.0
    out = jax.ShapeDtypeStruct((n_e, d, f2 // 2), BF16)
    return pl.pallas_call(
        _split_even_odd_kernel,
        grid=(n_e, d // tk),
        in_specs=[pl.BlockSpec((None, tk, f2), lambda e, i: (e, i, 0)), _full((SPLIT_W, SPLIT_W))],
        out_specs=[pl.BlockSpec((None, tk, f2 // 2), lambda e, i: (e, i, 0))] * 2,
        out_shape=[out, out],
        compiler_params=_cparams("parallel", "parallel"),
        name="split_even_odd",
    )(w, jnp.asarray(perm).astype(BF16))


def _moe_ffn_kernel(be_ref, used_ref, x_ref, w1g_ref, w1l_ref, b1g_ref, b1l_ref, w2_ref, b2_ref, o_ref):
    i = pl.program_id(0)
    rows = x_ref.shape[0]

    @pl.when(i * rows < used_ref[0])
    def _():
        x = x_ref[...].astype(BF16)
        glu = jnp.minimum(jnp.dot(x, w1g_ref[...], preferred_element_type=F32) + b1g_ref[...], SWIGLU_LIMIT)
        lin = jnp.clip(jnp.dot(x, w1l_ref[...], preferred_element_type=F32) + b1l_ref[...],
                       -SWIGLU_LIMIT, SWIGLU_LIMIT)
        act = glu * _sigmoid(SWIGLU_ALPHA * glu) * (lin + 1.0)
        o_ref[...] = _bdot(act, w2_ref[...]) + b2_ref[...]

    @pl.when(i * rows >= used_ref[0])
    def _():
        o_ref[...] = jnp.zeros(o_ref.shape, F32)


def moe_ffn(block_expert, n_used, x_rows, lw, rows):
    n_rows, d = x_rows.shape
    n_blocks = n_rows // rows
    d_ff = lw['moe_w2'].shape[1]
    wspec = lambda a, b_: pl.BlockSpec((None, a, b_), lambda i, be, nu: (be[i], 0, 0))
    grid_spec = pltpu.PrefetchScalarGridSpec(
        num_scalar_prefetch=2,
        grid=(n_blocks,),
        in_specs=[pl.BlockSpec((rows, d), lambda i, be, nu: (i, 0)),
                  wspec(d, d_ff), wspec(d, d_ff), wspec(1, d_ff), wspec(1, d_ff), wspec(d_ff, d), wspec(1, d)],
        out_specs=pl.BlockSpec((rows, d), lambda i, be, nu: (i, 0)))
    return pl.pallas_call(
        _moe_ffn_kernel,
        grid_spec=grid_spec,
        out_shape=jax.ShapeDtypeStruct((n_rows, d), F32),
        compiler_params=_cparams("arbitrary"),
        name="moe_ffn",
    )(block_expert, n_used, x_rows, lw['moe_w1g'], lw['moe_w1l'], lw['moe_b1g'], lw['moe_b1l'],
      lw['moe_w2'], lw['moe_b2'])


def _combine_kernel(x_ref, y_ref, gate_ref, gt2_ref, o_ref):
    gate = gate_ref[...]
    y = jnp.zeros(x_ref.shape, F32)
    for k in range(TOP_K):
        y = y + gate[:, k:k + 1] * y_ref[k]
    o_ref[...] = x_ref[...] + gt2_ref[...] * y


def combine(x1, t_len, y4, gate, gt2):
    n, d = x1.shape
    tm = min(n, 256)
    gt2_3, gt2_spec = _mod_rows(gt2, t_len, tm)
    return pl.pallas_call(
        _combine_kernel,
        grid=(n // tm,),
        in_specs=[pl.BlockSpec((tm, d), lambda i: (i, 0)), pl.BlockSpec((TOP_K, tm, d), lambda i: (0, i, 0)),
                  pl.BlockSpec((tm, LANES), lambda i: (i, 0)), gt2_spec],
        out_specs=pl.BlockSpec((tm, d), lambda i: (i, 0)),
        out_shape=jax.ShapeDtypeStruct((n, d), F32),
        compiler_params=_cparams("parallel"),
        name="combine",
    )(x1, y4, gate, gt2_3)


def moe(x1, t_len, h2, top_idx, gate, gt2, lw):
    n, d = h2.shape
    rows = 256 if n * TOP_K >= 8192 else 64
    a_n = n * TOP_K
    e_flat = top_idx[:, :TOP_K].reshape(-1)
    onehot = (e_flat[:, None] == jnp.arange(N_EXPERTS, dtype=I32)[None, :]).astype(I32)
    ranks = jnp.cumsum(onehot, axis=0) - onehot
    rank = jnp.sum(ranks * onehot, axis=1)
    counts = jnp.sum(onehot, axis=0)
    padded = (counts + rows - 1) // rows * rows
    pends = jnp.cumsum(padded)
    pstarts = pends - padded
    dest = (pstarts[e_flat] + rank).astype(I32)
    n_blocks = -(-a_n // rows) + N_EXPERTS
    tok_flat = jnp.repeat(jnp.arange(n, dtype=I32), TOP_K)
    row_tok = jnp.zeros(n_blocks * rows, I32).at[dest].set(tok_flat)
    block_start = jnp.arange(n_blocks, dtype=I32) * rows
    block_expert = jnp.minimum(jnp.sum((pends[None, :] <= block_start[:, None]).astype(I32), axis=1),
                               N_EXPERTS - 1).astype(I32)
    x_rows = h2[row_tok]
    y_rows = moe_ffn(block_expert, pends[-1:].astype(I32), x_rows, lw, rows)
    y4 = y_rows[dest.reshape(n, TOP_K).T]
    return combine(x1, t_len, y4, gate, gt2)


def _t5_bucket(dist):
    max_exact = REL_BUCKETS // 2
    d = jnp.maximum(dist, 0)
    large = max_exact + (jnp.log(jnp.maximum(d, 1).astype(F32) / max_exact)
                         / math.log(REL_MAX_DIST / max_exact) * (REL_BUCKETS - max_exact)).astype(I32)
    large = jnp.minimum(large, REL_BUCKETS - 1)
    return jnp.where(d < max_exact, d, large)


def _seg_matrix(value):
    head = np.arange(HW) // HEAD_DIM
    return jnp.asarray((head[:, None] == head[None, :]).astype(np.float32) * value)


def _pack_layer_weights(l, w):
    d = D_MODEL
    parts = jnp.split(w['w_in'][l], np.cumsum(np.array(IN_WIDTHS))[:-1].tolist(), axis=1)
    p_rwkv, pq, pk, pv, pqi, pki, pwi, rq, rk, rv, rg, pgate = parts
    zeros = lambda n: jnp.zeros((d, n), F32)
    w_in = jnp.concatenate([p_rwkv, zeros(RWKV_PAD_W - RWKV_PROJ_W), pki, pwi, zeros(LANES - HEAD_DIM - N_HEADS),
                            pq, pk, pv, pqi, rq, rk, rv, rg, pgate], axis=1).astype(BF16)
    lora = jnp.zeros((3, LORA_W, HW), F32)
    lora = lora.at[0, 0:DECAY_LORA].set(w['rwkv_w2'][l])
    lora = lora.at[1, DECAY_LORA:DECAY_LORA + AAA_LORA].set(w['rwkv_a2'][l])
    lora = lora.at[2, DECAY_LORA + AAA_LORA:DECAY_LORA + AAA_LORA + GATE_LORA].set(w['rwkv_g2'][l])
    vecs = jnp.zeros((8, HW), F32)
    for i, name in enumerate(('rwkv_w0', 'rwkv_a0', 'rwkv_k_k', 'rwkv_k_a', 'rwkv_r_k')):
        vecs = vecs.at[i].set(w[name][l])
    pad_lanes = lambda v, fill=0.0: jnp.concatenate(
        [v, jnp.full((LANES - v.shape[0],), fill, F32)]).reshape(1, LANES)
    b1 = w['moe_b1'][l]
    return dict(
        w_in=w_in, norm1_g=w['norm1_g'][l],
        mu=jnp.pad(w['rwkv_mu'][l], (0, RWKV_PAD_W - RWKV_PROJ_W)).reshape(1, RWKV_PAD_W),
        lora=lora.astype(BF16), vecs=vecs,
        qg=jnp.tile(w['q_norm_g'][l], N_HEADS).reshape(1, HW), kg=jnp.tile(w['k_norm_g'][l], N_HEADS).reshape(1, HW),
        ig=pad_lanes(w['idx_k_norm_g'][l]), ib=pad_lanes(w['idx_k_norm_b'][l]),
        ret_gn_g=w['ret_gn_g'][l],
        ln_g=w['rwkv_ln_g'][l].reshape(1, HW), ln_b=w['rwkv_ln_b'][l].reshape(1, HW),
        w_up_a=w['w_up_rwkv'][l].astype(BF16), w_up_b=w['w_up_att'][l].astype(BF16),
        w_up_c=w['w_up_ret'][l].astype(BF16), w_out=w['w_out'][l].astype(BF16),
        norm2_g=w['norm2_g'][l].reshape(1, d),
        router_w=jnp.pad(w['router_w'][l], ((0, 0), (0, LANES - N_EXPERTS))),
        router_b=pad_lanes(w['router_b'][l], NEG_INF),
        moe_w1g=w['moe_w1g'][l], moe_w1l=w['moe_w1l'][l],
        moe_b1g=b1[:, None, 0::2], moe_b1l=b1[:, None, 1::2],
        moe_w2=w['moe_w2'][l].astype(BF16), moe_b2=w['moe_b2'][l][:, None, :],
    )


def _attend_prompt(p3, qn, kn, kiw, rel_bias, tri):
    b, s_len, _ = p3.shape
    qb = 128
    nb = s_len // qb
    n_sel = max(1, min(TOPK_KEYS, s_len // 4))
    scale = HEAD_DIM ** -0.5
    qi = p3[:, :, C_QI:C_QI + HW].reshape(b, nb, qb, N_HEADS, HEAD_DIM).transpose(0, 1, 3, 2, 4)
    qi_stack = (qi * scale).reshape(b, nb, N_HEADS * qb, HEAD_DIM).astype(BF16)
    kiw3 = kiw.reshape(b, s_len, LANES)
    mask = dsa_select(qi_stack, kiw3[:, :, :HEAD_DIM].astype(BF16), kiw3, tri, n_sel)
    r = jnp.arange(qb)[:, None]
    c = jnp.arange(ATT_NEAR)[None, :]
    dist = jnp.stack([r - c, qb + r - c])
    toe = rel_bias[_t5_bucket(dist)].transpose(0, 3, 1, 2)
    toe = toe - rel_bias[REL_BUCKETS - 1][None, :, None, None]
    return dsa_attn((qn.reshape(b, s_len, HW) * scale).astype(BF16), kn.reshape(b, s_len, HW).astype(BF16),
                    p3[:, :, C_V:C_V + HW].astype(BF16), mask, toe)


def _attend_sample(p3, qn, kn, kiw, rel_bias, tri, cache_k_t, cache_v_t, cache_ki_t, layer, page_table):
    db, t_new, _ = p3.shape
    page = cache_ki_t.shape[3]
    n_pages = page_table.shape[1]
    assert page == LANES and page >= REL_MAX_DIST and t_new <= page
    n_sel = max(1, min(TOPK_KEYS, (n_pages * page + t_new) // 4))
    scale = HEAD_DIM ** -0.5
    rows = N_HEADS * t_new
    qi = p3[:, :, C_QI:C_QI + HW].reshape(db, t_new, N_HEADS, HEAD_DIM).transpose(0, 2, 1, 3)
    qi_rows = (qi * scale).reshape(db, rows, HEAD_DIM).astype(BF16)
    kiw3 = kiw.reshape(db, t_new, LANES)
    wcol = kiw3[:, :, HEAD_DIM:HEAD_DIM + N_HEADS].transpose(0, 2, 1).reshape(db, rows, 1)
    wcol = jnp.broadcast_to(wcol, (db, rows, LANES))
    new_t = lambda a: jnp.pad(a, ((0, 0), (0, page - t_new), (0, 0))).transpose(0, 2, 1)
    t_of_row = jnp.tile(jnp.arange(t_new), N_HEADS)[:, None]
    j = jnp.arange(page)[None, :]
    dist = jnp.stack([page + t_of_row - j, t_of_row - j])
    h_of_row = jnp.repeat(jnp.arange(N_HEADS), t_new)
    bias_tab = (rel_bias[_t5_bucket(dist), h_of_row[None, :, None]]
                - rel_bias[REL_BUCKETS - 1][h_of_row][None, :, None])
    return dsa_sample(page_table, layer, qi_rows, wcol, qn.reshape(db, t_new, HW) * scale,
                      cache_ki_t, new_t(kiw3[:, :, :HEAD_DIM]),
                      cache_k_t, new_t(kn.reshape(db, t_new, HW)),
                      cache_v_t, new_t(p3[:, :, C_V:C_V + HW]),
                      bias_tab, tri, n_sel)


def _layer_group(x, mods, lw, consts, rel_bias, pos0, shift0, rwkv0, ret0, attend):
    b, t_len, d = x.shape
    n = b * t_len
    sh1, sc1, gt1, sh2, sc2, gt2 = mods
    x2 = x.reshape(n, d)
    p2 = norm_proj(x2, t_len, lw['norm1_g'], sc1, sh1, lw['w_in'])
    p3 = p2.reshape(b, t_len, PROJ_W)
    prev = jnp.pad(shift0, ((0, 0), (0, RWKV_PAD_W - RWKV_PROJ_W)))[:, None, :]
    r, w, k2, v, kk, kka, g, bv, shift_new = rwkv_prep(p3, prev, lw['mu'], lw['vecs'], lw['lora'],
                                                       consts['seg_ones'])
    tp = -(-t_len // RWKV_CHUNK) * RWKV_CHUNK
    pad_t = lambda a: a if tp == t_len else jnp.pad(a, ((0, 0), (0, tp - t_len), (0, 0)))
    s0 = rwkv0.reshape(b, HW // LANES, 2, HEAD_DIM, HEAD_DIM)
    zero = jnp.zeros_like(s0[:, :, 0])
    s0_bd = jnp.concatenate([jnp.concatenate([s0[:, :, 0], zero], axis=-1),
                             jnp.concatenate([zero, s0[:, :, 1]], axis=-1)], axis=-2)
    ra, s_fin = rwkv_chunked(pad_t(w), pad_t(kk), pad_t(kka), pad_t(k2), pad_t(r), pad_t(v), s0_bd)
    ra = ra[:, :t_len]
    rwkv_new = jnp.stack([s_fin[:, :, :HEAD_DIM, :HEAD_DIM], s_fin[:, :, HEAD_DIM:, HEAD_DIM:]],
                         axis=2).reshape(b, N_HEADS, HEAD_DIM, HEAD_DIM)
    qn, kn, kiw = dsa_prep(p2, lw['qg'], lw['kg'], lw['ig'], lw['ib'], consts['seg_mean'])
    ob = attend(p3, qn, kn, kiw, rel_bias, consts['tri'])
    oc, ret_new = retention(p3, pos0, ret0.reshape(b, HW, RET_DV), lw['ret_gn_g'])
    lw = dict(lw, seg_mean=consts['seg_mean'])
    x1, h2, top_idx, gate = merge(x2, t_len, p2, ra.reshape(n, HW), bv.reshape(n, HW), g.reshape(n, HW),
                                  ob.reshape(n, HW), oc.reshape(n, RET_V_W), gt1, sc2, sh2, lw)
    x_out = moe(x1, t_len, h2, top_idx, gate, gt2, lw)
    return (x_out.reshape(b, t_len, d),
            kn.reshape(b, t_len, N_HEADS, HEAD_DIM),
            p3[:, :, C_V:C_V + HW].reshape(b, t_len, N_HEADS, HEAD_DIM),
            kiw.reshape(b, t_len, LANES)[:, :, :HEAD_DIM],
            rwkv_new,
            shift_new[:, 0, :RWKV_PROJ_W],
            ret_new.reshape(b, N_HEADS, HEAD_DIM, RET_DV))


def kernel(x_prompt, x_sample, c_prompt, c_sample, cache_k, cache_v, cache_kidx, state_rwkv, state_rwkv_shift, state_ret, page_table, rel_bias, ada_w, ada_b, norm1_g, norm2_g, w_in, rwkv_mu, rwkv_w0, rwkv_w2, rwkv_a0, rwkv_a2, rwkv_g2, rwkv_k_k, rwkv_k_a, rwkv_r_k, rwkv_ln_g, rwkv_ln_b, q_norm_g, k_norm_g, idx_k_norm_g, idx_k_norm_b, ret_gn_g, w_up_rwkv, w_up_att, w_up_ret, w_out, router_w, router_b, moe_w1, moe_b1, moe_w2, moe_b2):
    weights = dict(norm1_g=norm1_g, norm2_g=norm2_g, w_in=w_in, rwkv_mu=rwkv_mu, rwkv_w0=rwkv_w0,
                   rwkv_w2=rwkv_w2, rwkv_a0=rwkv_a0, rwkv_a2=rwkv_a2, rwkv_g2=rwkv_g2, rwkv_k_k=rwkv_k_k,
                   rwkv_k_a=rwkv_k_a, rwkv_r_k=rwkv_r_k, rwkv_ln_g=rwkv_ln_g, rwkv_ln_b=rwkv_ln_b,
                   q_norm_g=q_norm_g, k_norm_g=k_norm_g, idx_k_norm_g=idx_k_norm_g,
                   idx_k_norm_b=idx_k_norm_b, ret_gn_g=ret_gn_g, w_up_rwkv=w_up_rwkv, w_up_att=w_up_att,
                   w_up_ret=w_up_ret, w_out=w_out, router_w=router_w, router_b=router_b,
                   moe_b1=moe_b1, moe_w2=moe_w2, moe_b2=moe_b2)
    depth = w_in.shape[0]
    w1_shape = moe_w1.shape
    w1g, w1l = split_even_odd(moe_w1.reshape((-1,) + w1_shape[2:]))
    weights['moe_w1g'] = w1g.reshape(w1_shape[:3] + (w1_shape[3] // 2,))
    weights['moe_w1l'] = w1l.reshape(w1_shape[:3] + (w1_shape[3] // 2,))
    bp, dbatch = x_prompt.shape[0], x_sample.shape[0]
    past = page_table.shape[1] * cache_k.shape[2]
    dt = x_prompt.dtype
    consts = dict(seg_ones=_seg_matrix(1.0), seg_mean=_seg_matrix(1.0 / HEAD_DIM),
                  tri=jnp.asarray(np.triu(np.ones((LANES, LANES), np.float32), 1)).astype(BF16))
    n_l, n_pool, page = cache_k.shape[:3]
    cache_k_t = cache_k.transpose(0, 1, 3, 4, 2).reshape(n_l, n_pool, HW, page)
    cache_v_t = cache_v.transpose(0, 1, 3, 4, 2).reshape(n_l, n_pool, HW, page)
    cache_ki_t = cache_kidx.transpose(0, 1, 3, 2)
    n_c = bp + dbatch
    c_all = jnp.pad(jnp.concatenate([c_prompt, c_sample], axis=0), ((0, -n_c % 8), (0, 0)))
    mods_all = ada_all(c_all, ada_w, ada_b)
    xp, xs = x_prompt, x_sample
    new_p = [[] for _ in range(6)]
    new_s = [[] for _ in range(6)]
    for l in range(depth):
        lw = _pack_layer_weights(l, weights)
        mods_p = jnp.split(mods_all[l, :bp], 6, axis=-1)
        mods_s = jnp.split(mods_all[l, bp:n_c], 6, axis=-1)
        xp, *st_p = _layer_group(xp, mods_p, lw, consts, rel_bias, 0,
                                 jnp.zeros((bp, RWKV_PROJ_W), dt),
                                 jnp.zeros((bp, N_HEADS, HEAD_DIM, HEAD_DIM), dt),
                                 jnp.zeros((bp, N_HEADS, HEAD_DIM, RET_DV), dt),
                                 _attend_prompt)
        attend_s = functools.partial(_attend_sample, cache_k_t=cache_k_t, cache_v_t=cache_v_t,
                                     cache_ki_t=cache_ki_t, layer=l, page_table=page_table)
        xs, *st_s = _layer_group(xs, mods_s, lw, consts, rel_bias, past, state_rwkv_shift[l], state_rwkv[l],
                                 state_ret[l], attend_s)
        for lst, a in zip(new_p, st_p):
            lst.append(a)
        for lst, a in zip(new_s, st_s):
            lst.append(a)
    return (xp, xs, *[jnp.stack(a) for a in new_p], *[jnp.stack(a) for a in new_s])
```

```python
import functools
import math

import numpy as np
import jax
import jax.numpy as jnp
from jax import lax
from jax.experimental import pallas as pl
from jax.experimental.pallas import tpu as pltpu

F32 = jnp.float32
BF16 = jnp.bfloat16
I32 = jnp.int32

D_MODEL = 1024
HEAD_DIM = 64
N_HEADS = 8
HW = N_HEADS * HEAD_DIM
LANES = 128
DECAY_LORA, AAA_LORA, GATE_LORA = 64, 64, 160
RWKV_PROJ_W = 3 * HW + DECAY_LORA + AAA_LORA + GATE_LORA
RWKV_PAD_W = 1920
LORA_W = RWKV_PAD_W - 3 * HW
RWKV_GN_EPS = 64e-5
TOPK_KEYS = 256
REL_BUCKETS = 32
REL_MAX_DIST = 128
RET_DV = 2 * HEAD_DIM
RET_V_W = N_HEADS * RET_DV
RET_CHUNK = 128
ROPE_BASE = 10000.0
N_EXPERTS = 32
TOP_K = 4
SWIGLU_LIMIT = 7.0
SWIGLU_ALPHA = 1.702
RMS_EPS = 1e-6
LN_EPS = 1e-5
NEG_INF = -1e30
IN_WIDTHS = (RWKV_PROJ_W, HW, HW, HW, HW, HEAD_DIM, N_HEADS, HW, HW, RET_V_W, RET_V_W, 3 * D_MODEL)

C_RWKV, C_KIWI, C_Q, C_K, C_V, C_QI = 0, 1920, 2048, 2560, 3072, 3584
C_RQ, C_RK, C_RV, C_RG, C_GATE, PROJ_W = 4096, 4608, 5120, 6144, 7168, 10240

VMEM_LIMIT = 56 * 1024 * 1024


def _cparams(*sem):
    return pltpu.CompilerParams(dimension_semantics=sem, vmem_limit_bytes=VMEM_LIMIT)


def _bdot(a, b):
    return jnp.dot(a.astype(BF16), b.astype(BF16), preferred_element_type=F32)


def _bdot_nt(a, b):
    return lax.dot_general(a.astype(BF16), b.astype(BF16), (((1,), (1,)), ((), ())),
                           preferred_element_type=F32)


def _hdot(a, b):
    return jnp.dot(a, b, precision=lax.Precision.HIGHEST, preferred_element_type=F32)


def _split_bf16(a):
    hi = a.astype(BF16)
    return hi, (a - hi.astype(F32)).astype(BF16)


def _mdot(a, b, dims=(((1,), (0,)), ((), ()))):
    ah, al = _split_bf16(a)
    bh, bl = _split_bf16(b)
    d = lambda x, y: lax.dot_general(x, y, dims, preferred_element_type=F32)
    return d(ah, bh) + (d(ah, bl) + d(al, bh))


def _mdot_nt(a, b):
    return _mdot(a, b, (((1,), (1,)), ((), ())))


def _sigmoid(x):
    return 1.0 / (1.0 + jnp.exp(-x))


def _full(shape):
    n = len(shape)
    return pl.BlockSpec(shape, lambda *_: (0,) * n)


def _ada_kernel(c_ref, w_ref, b_ref, o_ref):
    c = c_ref[...]
    o_ref[...] = _bdot(c * _sigmoid(c), w_ref[...]) + b_ref[...]


def ada_all(c_all, ada_w, ada_b):
    n_l, d, d6 = ada_w.shape
    bp = c_all.shape[0]
    tn = 1024
    return pl.pallas_call(
        _ada_kernel,
        grid=(n_l, d6 // tn),
        in_specs=[pl.BlockSpec((bp, d), lambda l, j: (0, 0)),
                  pl.BlockSpec((None, d, tn), lambda l, j: (l, 0, j)),
                  pl.BlockSpec((None, 1, tn), lambda l, j: (l, 0, j))],
        out_specs=pl.BlockSpec((None, bp, tn), lambda l, j: (l, 0, j)),
        out_shape=jax.ShapeDtypeStruct((n_l, bp, d6), F32),
        compiler_params=_cparams("parallel", "parallel"),
        name="ada",
    )(c_all, ada_w, ada_b.reshape(n_l, 1, d6))


def _mod_rows(m, t_len, tm):
    b, d = m.shape
    if t_len % tm == 0:
        per = t_len // tm
        return m[:, None, :], pl.BlockSpec((None, 1, d), lambda i, *_: (i // per, 0, 0))
    assert tm % t_len == 0 and (b * t_len) % tm == 0
    arr = jnp.repeat(m, t_len, axis=0).reshape(b * t_len // tm, tm, d)
    return arr, pl.BlockSpec((None, tm, d), lambda i, *_: (i, 0, 0))


def _norm_proj_kernel(x_ref, g_ref, sc_ref, sh_ref, w_ref, o_ref, h_scr):
    @pl.when(pl.program_id(1) == 0)
    def _():
        x = x_ref[...]
        y = x * lax.rsqrt(jnp.mean(x * x, axis=-1, keepdims=True) + RMS_EPS) * g_ref[...]
        h_scr[...] = (y * (1.0 + sc_ref[...]) + sh_ref[...]).astype(BF16)

    o_ref[...] = jnp.dot(h_scr[...], w_ref[...], preferred_element_type=F32)


def norm_proj(x2, t_len, g, sc, sh, w):
    n, d = x2.shape
    pw = w.shape[1]
    tm = min(n, 1024)
    tn = 1024
    sc3, sc_spec = _mod_rows(sc, t_len, tm)
    sh3, sh_spec = _mod_rows(sh, t_len, tm)
    return pl.pallas_call(
        _norm_proj_kernel,
        grid=(n // tm, pw // tn),
        in_specs=[pl.BlockSpec((tm, d), lambda i, j: (i, 0)),
                  pl.BlockSpec((1, d), lambda i, j: (0, 0)),
                  sc_spec, sh_spec,
                  pl.BlockSpec((d, tn), lambda i, j: (0, j))],
        out_specs=pl.BlockSpec((tm, tn), lambda i, j: (i, j)),
        out_shape=jax.ShapeDtypeStruct((n, pw), F32),
        scratch_shapes=[pltpu.VMEM((tm, d), BF16)],
        compiler_params=_cparams("parallel", "arbitrary"),
        name="norm_proj",
    )(x2, g.reshape(1, d), sc3, sh3, w)


def _rwkv_prep_kernel(p_ref, prev_ref, mu_ref, vec_ref, lw_ref, e_ref,
                      r_o, w_o, k_o, v_o, kk_o, kka_o, g_o, bv_o, shift_o, carry):
    j = pl.program_id(1)
    p = p_ref[...]
    tt = p.shape[0]
    first = jnp.where(j == 0, prev_ref[...], carry[...])
    rows = lax.broadcasted_iota(I32, p.shape, 0)
    p_prev = jnp.where(rows == 0, first, pltpu.roll(p, 1, 0))
    carry[...] = p[tt - 1:tt, :]
    shift_o[...] = p[tt - 1:tt, :]
    xm = p + (p_prev - p) * mu_ref[...]
    r, k, v, xl = xm[:, 0:HW], xm[:, HW:2 * HW], xm[:, 2 * HW:3 * HW], xm[:, 3 * HW:]
    w0, a0, k_k = vec_ref[0:1, :], vec_ref[1:2, :], vec_ref[2:3, :]
    k_a, r_k = vec_ref[3:4, :], vec_ref[4:5, :]
    z = -(w0 + _bdot(jnp.tanh(xl), lw_ref[0]))
    softplus = jnp.maximum(z, 0.0) + jnp.log(1.0 + jnp.exp(-jnp.abs(z)))
    log_decay = -jnp.exp(-softplus - 0.5)
    a = _sigmoid(a0 + _bdot(xl, lw_ref[1]))
    g = _bdot(_sigmoid(xl), lw_ref[2])
    seg_ones = e_ref[...]
    kk = k * k_k
    kk = kk / jnp.maximum(jnp.sqrt(_hdot(kk * kk, seg_ones)), 1e-12)
    k2 = k * (1.0 + (a - 1.0) * k_a)
    r_o[...] = r
    w_o[...] = log_decay
    k_o[...] = k2
    v_o[...] = v
    kk_o[...] = kk
    kka_o[...] = kk * a
    g_o[...] = g
    bv_o[...] = _hdot(r * k2 * r_k, seg_ones) * v


def rwkv_prep(p3, prev, mu, vecs, lora_w, seg_ones):
    b, t_len, _ = p3.shape
    tt = min(t_len, 256)
    tok = pl.BlockSpec((None, tt, HW), lambda i, j: (i, j, 0))
    row = pl.BlockSpec((None, 1, RWKV_PAD_W), lambda i, j: (i, 0, 0))
    tok_shape = jax.ShapeDtypeStruct((b, t_len, HW), F32)
    return pl.pallas_call(
        _rwkv_prep_kernel,
        grid=(b, t_len // tt),
        in_specs=[pl.BlockSpec((None, tt, RWKV_PAD_W), lambda i, j: (i, j, 0)), row,
                  _full((1, RWKV_PAD_W)), _full((8, HW)), _full((3, LORA_W, HW)), _full((HW, HW))],
        out_specs=[tok] * 8 + [row],
        out_shape=[tok_shape] * 8 + [jax.ShapeDtypeStruct((b, 1, RWKV_PAD_W), F32)],
        scratch_shapes=[pltpu.VMEM((1, RWKV_PAD_W), F32)],
        compiler_params=_cparams("parallel", "arbitrary"),
        name="rwkv_prep",
    )(p3, prev, mu, vecs, lora_w, seg_ones)


RWKV_CHUNK = 64
SOLVE_BLOCK = 16


def _rwkv_chunk_kernel(lw_ref, kk_ref, kka_ref, k_ref, r_ref, v_ref, s0_ref, tril_ref, o_ref, sfin_ref, st):
    c = pl.program_id(1)

    @pl.when(c == 0)
    def _():
        st[...] = s0_ref[...]

    n = RWKV_CHUNK
    lw = lw_ref[...]
    lc = _hdot(tril_ref[...], lw)
    lc_last = lc[n - 1:n, :]
    p_inv = jnp.exp(-lc)
    p_rem = jnp.exp(lc_last - lc)
    a_hat = -kk_ref[...] * jnp.exp(lc - lw)
    r_hat = r_ref[...] * jnp.exp(lc)
    b_hat = kka_ref[...] * p_inv
    k_hat = k_ref[...] * p_inv
    b_til = kka_ref[...] * p_rem
    k_til = k_ref[...] * p_rem
    v = v_ref[...]
    p_last = jnp.exp(lc_last)
    row = lax.broadcasted_iota(I32, (2 * n, 2 * n), 0)
    col = lax.broadcasted_iota(I32, (2 * n, 2 * n), 1)
    strict = (col % n) < (row % n)
    incl = (col % n) <= (row % n)
    same_block = (row // SOLVE_BLOCK) == (col // SOLVE_BLOCK)
    same_head = (row // n) == (col // n)
    eye = jnp.where(row == col, 1.0, 0.0)
    low = lax.broadcasted_iota(I32, (n, LANES), 1) < HEAD_DIM
    for p in range(HW // LANES):
        lanes = slice(p * LANES, (p + 1) * LANES)

        def stacked(x):
            xp = x[:, lanes]
            return jnp.concatenate([jnp.where(low, xp, 0.0), jnp.where(low, 0.0, xp)], axis=0)

        ar = jnp.concatenate([stacked(a_hat), stacked(r_hat)], axis=0)
        vs = stacked(v)
        gb = _bdot_nt(ar, stacked(b_hat))
        gk = _bdot_nt(ar, stacked(k_hat))
        l_ab = jnp.where(strict, gb[0:2 * n], 0.0)
        l_ak = jnp.where(strict, gk[0:2 * n], 0.0)
        n_rb = jnp.where(incl, gb[2 * n:4 * n], 0.0)
        n_rk = jnp.where(incl, gk[2 * n:4 * n], 0.0)
        state = st[p]
        t0 = _bdot_nt(ar, state)
        rhs = t0[0:2 * n] + _bdot(l_ak, vs)
        l_d = jnp.where(same_block, l_ab, 0.0)
        l_o = l_ab - l_d
        l_2 = _mdot(l_d, l_d)
        q = eye + l_d
        q = q + _mdot(q, l_2)
        l_4 = _mdot(l_2, l_2)
        q = q + _mdot(q, l_4)
        q = q + _mdot(q, _mdot(l_4, l_4))
        w_o = _mdot(q, l_o)
        z = _mdot(q, rhs)
        y = z + _mdot(_mdot(w_o, w_o), z)
        sa = y + _mdot(w_o, y)
        out = t0[2 * n:4 * n] + _bdot(jnp.concatenate([n_rb, n_rk], axis=1), jnp.concatenate([sa, vs], axis=0))
        o_ref[:, lanes] = out[0:n] + out[n:2 * n]
        x = jnp.concatenate([sa[0:n] + sa[n:2 * n], v[:, lanes]], axis=0)
        upd = _bdot(x.T, jnp.concatenate([b_til[:, lanes], k_til[:, lanes]], axis=0))
        st[p] = state * p_last[:, lanes] + jnp.where(same_head, upd, 0.0)
    sfin_ref[...] = st[...]


def rwkv_chunked(lw, kk, kka, k2, r, v, s0_bd):
    b, tp, _ = lw.shape
    n_pair = HW // LANES
    tok = pl.BlockSpec((None, RWKV_CHUNK, HW), lambda i, j: (i, j, 0))
    state = pl.BlockSpec((None, n_pair, LANES, LANES), lambda i, j: (i, 0, 0, 0))
    tril = jnp.asarray(np.tril(np.ones((RWKV_CHUNK, RWKV_CHUNK), np.float32)))
    return pl.pallas_call(
        _rwkv_chunk_kernel,
        grid=(b, tp // RWKV_CHUNK),
        in_specs=[tok] * 6 + [state, _full((RWKV_CHUNK, RWKV_CHUNK))],
        out_specs=[tok, state],
        out_shape=[jax.ShapeDtypeStruct((b, tp, HW), F32), jax.ShapeDtypeStruct((b, n_pair, LANES, LANES), F32)],
        scratch_shapes=[pltpu.VMEM((n_pair, LANES, LANES), F32)],
        compiler_params=_cparams("parallel", "arbitrary"),
        name="rwkv_chunked",
    )(lw, kk, kka, k2, r, v, s0_bd, tril)


def _dsa_prep_kernel(kiwi_ref, qk_ref, qg_ref, kg_ref, ig_ref, ib_ref, e_ref, qn_o, kn_o, kiw_o):
    seg_mean = e_ref[...]
    q, k = qk_ref[:, 0:HW], qk_ref[:, HW:2 * HW]
    qn_o[...] = q * lax.rsqrt(_hdot(q * q, seg_mean) + RMS_EPS) * qg_ref[...]
    kn_o[...] = k * lax.rsqrt(_hdot(k * k, seg_mean) + RMS_EPS) * kg_ref[...]
    x = kiwi_ref[...]
    is_key = lax.broadcasted_iota(I32, x.shape, 1) < HEAD_DIM
    mu = jnp.sum(jnp.where(is_key, x, 0.0), axis=-1, keepdims=True) / HEAD_DIM
    dev = jnp.where(is_key, x - mu, 0.0)
    var = jnp.sum(dev * dev, axis=-1, keepdims=True) / HEAD_DIM
    kin = dev * lax.rsqrt(var + LN_EPS) * ig_ref[...] + ib_ref[...]
    kiw_o[...] = jnp.where(is_key, kin, x * N_HEADS ** -0.5)


def dsa_prep(p2, qg, kg, ig, ib, seg_mean):
    n = p2.shape[0]
    tm = min(n, 512)
    return pl.pallas_call(
        _dsa_prep_kernel,
        grid=(n // tm,),
        in_specs=[pl.BlockSpec((tm, LANES), lambda i: (i, C_KIWI // LANES)),
                  pl.BlockSpec((tm, 2 * HW), lambda i: (i, C_Q // (2 * HW))),
                  _full((1, HW)), _full((1, HW)), _full((1, LANES)), _full((1, LANES)), _full((HW, HW))],
        out_specs=[pl.BlockSpec((tm, HW), lambda i: (i, 0)), pl.BlockSpec((tm, HW), lambda i: (i, 0)),
                   pl.BlockSpec((tm, LANES), lambda i: (i, 0))],
        out_shape=[jax.ShapeDtypeStruct((n, HW), F32), jax.ShapeDtypeStruct((n, HW), F32),
                   jax.ShapeDtypeStruct((n, LANES), F32)],
        compiler_params=_cparams("parallel"),
        name="dsa_prep",
    )(p2, p2, qg, kg, ig, ib, seg_mean)


INT_MIN = -2 ** 31
SEL_CHUNK = 512


def _order_key(score):
    bits = lax.bitcast_convert_type(score + 0.0, I32)
    return bits ^ ((bits >> 31) & 0x7FFFFFFF)


def _count_ge(key_ref, n_chunks, cand, rows, strict=False):
    def body(c, acc):
        blk = key_ref[:, pl.ds(pl.multiple_of(c * SEL_CHUNK, SEL_CHUNK), SEL_CHUNK)]
        hit = (blk > cand) if strict else (blk >= cand)
        ones = jnp.where(hit, 1.0, 0.0)
        for g in range(SEL_CHUNK // LANES):
            acc = acc + ones[:, g * LANES:(g + 1) * LANES]
        return acc
    acc = lax.fori_loop(0, n_chunks, body, jnp.zeros((rows, LANES), F32))
    return jnp.sum(acc, axis=1, keepdims=True)


def _kth_largest_key(key_ref, n_chunks, n_sel, rows):
    cnt0 = _count_ge(key_ref, n_chunks, jnp.zeros((rows, 1), I32), rows)
    thr = jnp.where(cnt0 >= n_sel, 0, INT_MIN).astype(I32)
    n_keys = jnp.full((rows, 1), n_chunks * SEL_CHUNK, I32).astype(F32)
    cnt_thr = jnp.where(cnt0 >= n_sel, cnt0, n_keys)

    def bit_body(it, carry):
        def refine(carry):
            thr, cnt_thr = carry
            cand = thr + jnp.left_shift(jnp.int32(1), 30 - it)
            cnt = _count_ge(key_ref, n_chunks, cand, rows)
            keep = cnt >= n_sel
            return jnp.where(keep, cand, thr), jnp.where(keep, cnt, cnt_thr)

        settled = jnp.max(carry[1]) <= n_sel
        return lax.cond(settled, lambda c: c, refine, carry)

    return lax.fori_loop(0, 31, bit_body, (thr, cnt_thr))[0]


def _select_into(key_ref, valid_fn, n_chunks, n_sel, rows, tri_ref, write_fn):
    thr = _kth_largest_key(key_ref, n_chunks, n_sel, rows)
    n_gt = _count_ge(key_ref, n_chunks, thr, rows, strict=True)
    need = n_sel - n_gt

    def chunk(c):
        return key_ref[:, pl.ds(pl.multiple_of(c * SEL_CHUNK, SEL_CHUNK), SEL_CHUNK)]

    def count_ties(c, acc):
        return acc + jnp.sum(jnp.where((chunk(c) == thr) & valid_fn(c), 1.0, 0.0), axis=1, keepdims=True)

    n_eq = lax.fori_loop(0, n_chunks, count_ties, jnp.zeros((rows, 1), F32))
    surplus = jnp.max(n_eq - need) > 0.0

    @pl.when(jnp.logical_not(surplus))
    def _():
        def body(c, carry):
            blk = chunk(c)
            write_fn(c, (blk >= thr) & valid_fn(c))
            return carry
        lax.fori_loop(0, n_chunks, body, 0)

    @pl.when(surplus)
    def _():
        def body(c, seen):
            blk = chunk(c)
            val = valid_fn(c)
            tie = jnp.where((blk == thr) & val, 1.0, 0.0)
            parts = []
            for g in range(SEL_CHUNK // LANES):
                tg = tie[:, g * LANES:(g + 1) * LANES]
                before = _bdot(tg, tri_ref[...]) + seen
                parts.append(jnp.where(before < need, tg, 0.0))
                seen = seen + jnp.sum(tg, axis=1, keepdims=True)
            take = jnp.concatenate(parts, axis=1) > 0.0
            write_fn(c, ((blk > thr) & val) | take)
            return seen
        lax.fori_loop(0, n_chunks, body, jnp.zeros((rows, 1), F32))


def _dsa_select_kernel(qi_ref, ki_ref, wi_ref, tri_ref, m_ref, key_scr, *, n_sel, qb):
    i = pl.program_id(1)
    n_chunks = (i * qb + qb + SEL_CHUNK - 1) // SEL_CHUNK
    m_ref[...] = jnp.zeros(m_ref.shape, m_ref.dtype)
    qpos = i * qb + lax.broadcasted_iota(I32, (qb, SEL_CHUNK), 0)
    lane_pos = lax.broadcasted_iota(I32, (qb, SEL_CHUNK), 1)
    wi = wi_ref[...].astype(BF16).astype(F32)

    def causal(c):
        return lane_pos + c * SEL_CHUNK <= qpos

    def score_chunk(c, carry):
        off = pl.multiple_of(c * SEL_CHUNK, SEL_CHUNK)
        s = jnp.maximum(_bdot_nt(qi_ref[...], ki_ref[pl.ds(off, SEL_CHUNK), :]), 0.0)
        s = s.astype(BF16).astype(F32)
        acc = jnp.zeros((qb, SEL_CHUNK), F32)
        for h in range(N_HEADS):
            acc = acc + s[h * qb:(h + 1) * qb, :] * wi[:, HEAD_DIM + h:HEAD_DIM + h + 1]
        key_scr[:, pl.ds(off, SEL_CHUNK)] = _order_key(jnp.where(causal(c), acc, NEG_INF))
        return carry

    lax.fori_loop(0, n_chunks, score_chunk, 0)

    def write(c, sel):
        off = pl.multiple_of(c * SEL_CHUNK, SEL_CHUNK)
        m_ref[:, pl.ds(off, SEL_CHUNK)] = jnp.where(sel, 1, 0).astype(m_ref.dtype)

    _select_into(key_scr, causal, n_chunks, n_sel, qb, tri_ref, write)


def dsa_select(qi_stack, ki, kiw3, tri, n_sel):
    b, nb, rows, _ = qi_stack.shape
    qb = rows // N_HEADS
    s_len = ki.shape[1]
    return pl.pallas_call(
        functools.partial(_dsa_select_kernel, n_sel=n_sel, qb=qb),
        grid=(b, nb),
        in_specs=[pl.BlockSpec((None, None, rows, HEAD_DIM), lambda bi, i: (bi, i, 0, 0)),
                  pl.BlockSpec((None, s_len, HEAD_DIM), lambda bi, i: (bi, 0, 0)),
                  pl.BlockSpec((None, qb, LANES), lambda bi, i: (bi, i, 0)),
                  _full((LANES, LANES))],
        out_specs=pl.BlockSpec((None, qb, s_len), lambda bi, i: (bi, i, 0)),
        out_shape=jax.ShapeDtypeStruct((b, s_len, s_len), jnp.int8),
        scratch_shapes=[pltpu.VMEM((qb, s_len), I32)],
        compiler_params=_cparams("parallel", "arbitrary"),
        name="dsa_select",
    )(qi_stack, ki, kiw3, tri)


ATT_CHUNK = 512
ATT_NEAR = 256
M_INIT = -1e29


def _softmax_step(carry, s, v, v_transposed=False):
    m, l, acc = carry
    m_new = jnp.maximum(m, jnp.max(s, axis=1, keepdims=True))
    alpha = jnp.exp(m - m_new)
    p = jnp.exp(s - m_new)
    pv = _bdot_nt(p, v) if v_transposed else _bdot(p, v)
    return m_new, l * alpha + jnp.sum(p, axis=1, keepdims=True), acc * alpha + pv


def _dsa_attn_kernel(q_ref, k_ref, v_ref, m_ref, toe_ref, o_ref, *, qb):
    i = pl.program_id(1)
    near_start = jnp.maximum(i - 1, 0) * qb
    n_far = (near_start + ATT_CHUNK - 1) // ATT_CHUNK
    n_pair = HW // LANES
    low = lax.broadcasted_iota(I32, (qb, LANES), 1) < HEAD_DIM
    kpos = lax.broadcasted_iota(I32, (qb, ATT_CHUNK), 1)
    q = q_ref[...]
    q2 = []
    for p in range(n_pair):
        qp = q[:, p * LANES:(p + 1) * LANES]
        zero = jnp.zeros_like(qp)
        q2.append(jnp.concatenate([jnp.where(low, qp, zero), jnp.where(low, zero, qp)], axis=0))

    def far_chunk(c, carry):
        off = pl.multiple_of(c * ATT_CHUNK, ATT_CHUNK)
        sel = m_ref[:, pl.ds(off, ATT_CHUNK)].astype(I32) != 0
        add = jnp.where(sel & (kpos + off < near_start), 0.0, NEG_INF)
        add2 = jnp.concatenate([add, add], axis=0)
        out = []
        for p in range(n_pair):
            lanes = slice(p * LANES, (p + 1) * LANES)
            s = _bdot_nt(q2[p], k_ref[pl.ds(off, ATT_CHUNK), lanes]) + add2
            out.append(_softmax_step(carry[p], s, v_ref[pl.ds(off, ATT_CHUNK), lanes]))
        return tuple(out)

    init = tuple((jnp.full((2 * qb, 1), M_INIT, F32), jnp.zeros((2 * qb, 1), F32),
                  jnp.zeros((2 * qb, LANES), F32)) for _ in range(n_pair))
    carry = lax.fori_loop(0, n_far, far_chunk, init)
    off = pl.multiple_of(near_start, qb)
    add = jnp.where(m_ref[:, pl.ds(off, ATT_NEAR)].astype(I32) != 0, 0.0, NEG_INF)
    for p in range(n_pair):
        lanes = slice(p * LANES, (p + 1) * LANES)
        bias2 = jnp.concatenate([toe_ref[2 * p] + add, toe_ref[2 * p + 1] + add], axis=0)
        s = _bdot_nt(q2[p], k_ref[pl.ds(off, ATT_NEAR), lanes]) + bias2
        _, l, acc = _softmax_step(carry[p], s, v_ref[pl.ds(off, ATT_NEAR), lanes])
        o = acc / l
        o_ref[:, lanes] = jnp.where(low, o[0:qb, :], o[qb:2 * qb, :])


def dsa_attn(q, k, v, mask, toe):
    b, s_len, _ = q.shape
    qb = 128
    kv = pl.BlockSpec((None, s_len, HW), lambda bi, i: (bi, 0, 0))
    return pl.pallas_call(
        functools.partial(_dsa_attn_kernel, qb=qb),
        grid=(b, s_len // qb),
        in_specs=[pl.BlockSpec((None, qb, HW), lambda bi, i: (bi, i, 0)), kv, kv,
                  pl.BlockSpec((None, qb, s_len), lambda bi, i: (bi, i, 0)),
                  pl.BlockSpec((None, N_HEADS, qb, ATT_NEAR), lambda bi, i: (jnp.minimum(i, 1), 0, 0, 0))],
        out_specs=pl.BlockSpec((None, qb, HW), lambda bi, i: (bi, i, 0)),
        out_shape=jax.ShapeDtypeStruct((b, s_len, HW), F32),
        compiler_params=_cparams("parallel", "arbitrary"),
        name="dsa_attn",
    )(q, k, v, mask, toe)


def _dsa_sample_kernel(pt_ref, qi_ref, wcol_ref, q_ref, *refs, n_pages, n_sel, t_new, page, group):
    cki, nki_ref = refs[0:group], refs[group]
    ck, nk_ref = refs[group + 1:2 * group + 1], refs[2 * group + 1]
    cv, nv_ref = refs[2 * group + 2:3 * group + 2], refs[3 * group + 2]
    bias_ref, tri_ref, o_ref, key_scr, sel_scr, m_scr, l_scr, acc_scr = refs[3 * group + 3:]
    phase = pl.program_id(1)
    g = pl.program_id(2)
    rows = N_HEADS * t_new
    gw = group * page
    new_off = n_pages * page
    n_chunks = -(-(n_pages + 1) * page // SEL_CHUNK)
    last_group = g == n_pages // group - 1

    def index_scores(ki_t):
        s = jnp.maximum(_bdot(qi_ref[...], ki_t), 0.0).astype(BF16).astype(F32)
        s = s * wcol_ref[:, 0:1].astype(BF16).astype(F32)
        acc = jnp.zeros((t_new, s.shape[1]), F32)
        for h in range(N_HEADS):
            acc = acc + s[h * t_new:(h + 1) * t_new, :]
        return acc

    @pl.when(jnp.logical_and(phase == 0, g == 0))
    def _():
        key_scr[...] = jnp.full(key_scr.shape, INT_MIN, I32)

    @pl.when(phase == 0)
    def _():
        ki_t = jnp.concatenate([r[...].astype(BF16) for r in cki], axis=1)
        key_scr[:, pl.ds(pl.multiple_of(g * gw, gw), gw)] = _order_key(index_scores(ki_t))

        @pl.when(last_group)
        def _():
            ok = lax.broadcasted_iota(I32, (t_new, page), 1) <= lax.broadcasted_iota(I32, (t_new, page), 0)
            key_scr[:, new_off:new_off + page] = _order_key(jnp.where(ok, index_scores(nki_ref[...]), NEG_INF))

    @pl.when(jnp.logical_and(phase == 1, g == 0))
    def _():
        pos = lax.broadcasted_iota(I32, (t_new, SEL_CHUNK), 1)
        qpos = new_off + lax.broadcasted_iota(I32, (t_new, SEL_CHUNK), 0)

        def valid(c):
            return pos + c * SEL_CHUNK <= qpos

        def write(c, sel):
            sel_scr[:, pl.ds(pl.multiple_of(c * SEL_CHUNK, SEL_CHUNK), SEL_CHUNK)] = jnp.where(sel, 0.0, NEG_INF)

        _select_into(key_scr, valid, n_chunks, n_sel, t_new, tri_ref, write)
        m_scr[...] = jnp.full(m_scr.shape, M_INIT, F32)
        l_scr[...] = jnp.zeros(l_scr.shape, F32)
        acc_scr[...] = jnp.zeros(acc_scr.shape, F32)

    @pl.when(phase == 1)
    def _():
        head_of_row = lax.broadcasted_iota(I32, (rows, HW), 0) // t_new
        own_head = head_of_row == lax.broadcasted_iota(I32, (rows, HW), 1) // HEAD_DIM
        qx = jnp.where(own_head, jnp.concatenate([q_ref[...]] * N_HEADS, axis=0), 0.0)

        def attend(k_t, v_t, add_rows, bias):
            s = _bdot(qx, k_t) + jnp.concatenate([add_rows] * N_HEADS, axis=0) + bias
            m, l, acc = _softmax_step((m_scr[...], l_scr[...], acc_scr[...]), s, v_t, v_transposed=True)
            m_scr[...] = m
            l_scr[...] = l
            acc_scr[...] = acc

        k_t = jnp.concatenate([r[...].astype(BF16) for r in ck], axis=1)
        v_t = jnp.concatenate([r[...].astype(BF16) for r in cv], axis=1)
        near = jnp.where(last_group, bias_ref[0], 0.0)
        bias = jnp.concatenate([jnp.zeros((rows, gw - page), F32), near], axis=1) if gw > page else near
        attend(k_t, v_t, sel_scr[:, pl.ds(pl.multiple_of(g * gw, gw), gw)], bias)

        @pl.when(last_group)
        def _():
            attend(nk_ref[...], nv_ref[...], sel_scr[:, new_off:new_off + page], bias_ref[1])
            o_full = jnp.where(own_head, acc_scr[...] / l_scr[...], 0.0)
            out = jnp.zeros((t_new, HW), F32)
            for h in range(N_HEADS):
                out = out + o_full[h * t_new:(h + 1) * t_new, :]
            o_ref[...] = out


def dsa_sample(page_table, layer, qi_rows, wcol, q3, cache_ki, new_ki, cache_k, new_k, cache_v, new_v, bias_tab, tri,
               n_sel):
    db, n_pages = page_table.shape
    page = cache_ki.shape[3]
    t_new = q3.shape[1]
    rows = N_HEADS * t_new
    group = math.gcd(n_pages, 8)
    n_groups = n_pages // group
    n_chunks = -(-(n_pages + 1) * page // SEL_CHUNK)

    def cache_spec(width, active_phase, idle_group, u):
        def index(b, ph, g, pt):
            use = jnp.where(ph == active_phase, g, idle_group) * group + u
            return (layer, pt[b * n_pages + use], 0, 0)
        return pl.BlockSpec((None, None, width, page), index)

    def per_b(shape):
        return pl.BlockSpec((None,) + shape, lambda b, ph, g, pt: (b, 0, 0))

    ki_specs = [cache_spec(HEAD_DIM, 0, n_groups - 1, u) for u in range(group)]
    kv_specs = [cache_spec(HW, 1, 0, u) for u in range(group)]
    grid_spec = pltpu.PrefetchScalarGridSpec(
        num_scalar_prefetch=1,
        grid=(db, 2, n_groups),
        in_specs=[per_b((rows, HEAD_DIM)), per_b((rows, LANES)), per_b((t_new, HW))]
                 + ki_specs + [per_b((HEAD_DIM, page))] + kv_specs + [per_b((HW, page))]
                 + kv_specs + [per_b((HW, page))]
                 + [pl.BlockSpec((2, rows, page), lambda b, ph, g, pt: (0, 0, 0)),
                    pl.BlockSpec((LANES, LANES), lambda b, ph, g, pt: (0, 0))],
        out_specs=pl.BlockSpec((None, t_new, HW), lambda b, ph, g, pt: (b, 0, 0)),
        scratch_shapes=[pltpu.VMEM((t_new, n_chunks * SEL_CHUNK), I32),
                        pltpu.VMEM((t_new, n_chunks * SEL_CHUNK), F32),
                        pltpu.VMEM((rows, 1), F32), pltpu.VMEM((rows, 1), F32), pltpu.VMEM((rows, HW), F32)])
    return pl.pallas_call(
        functools.partial(_dsa_sample_kernel, n_pages=n_pages, n_sel=n_sel, t_new=t_new, page=page, group=group),
        grid_spec=grid_spec,
        out_shape=jax.ShapeDtypeStruct((db, t_new, HW), F32),
        compiler_params=_cparams("parallel", "arbitrary", "arbitrary"),
        name="dsa_sample",
    )(page_table.reshape(-1), qi_rows, wcol, q3, *([cache_ki] * group), new_ki, *([cache_k] * group), new_k,
      *([cache_v] * group), new_v, bias_tab, tri)


def _retention_kernel(q_ref, k_ref, v_ref, g_ref, cos_ref, sin_ref, dmask_ref, qdec_ref, kdec_ref, cdec_ref,
                      gn_ref, s0_ref, o_ref, sfin_ref, st, *, rows):
    c = pl.program_id(1)

    @pl.when(c == 0)
    def _():
        st[...] = s0_ref[...]

    def padded(x):
        if x.shape[0] == RET_CHUNK:
            return x
        return jnp.concatenate([x, jnp.zeros((RET_CHUNK - x.shape[0], x.shape[1]), x.dtype)], axis=0)

    lane = lax.broadcasted_iota(I32, (RET_CHUNK, HW), 1)
    first_half = (lane % HEAD_DIM) < HEAD_DIM // 2

    def rope(x):
        partner = jnp.where(first_half, pltpu.roll(x, HW - HEAD_DIM // 2, 1), pltpu.roll(x, HEAD_DIM // 2, 1))
        return x * padded(cos_ref[...]) + partner * padded(sin_ref[...])

    q = rope(padded(q_ref[...]))
    k = rope(padded(k_ref[...])) * HEAD_DIM ** -0.5
    v = padded(v_ref[...])
    state = st[...]
    head_of_lane = lane // HEAD_DIM
    qdec = qdec_ref[...]
    o_parts = []
    for h in range(N_HEADS):
        qh = jnp.where(head_of_lane == h, q, 0.0)
        inner = _bdot_nt(qh, k) * dmask_ref[h]
        vh = v[:, h * RET_DV:(h + 1) * RET_DV]
        oh = _bdot(inner, vh) + _bdot(qh, state) * qdec[:, h:h + 1]
        mu = jnp.mean(oh, axis=-1, keepdims=True)
        dev = oh - mu
        var = jnp.mean(dev * dev, axis=-1, keepdims=True)
        o_parts.append(dev * lax.rsqrt(var + LN_EPS))
    o = jnp.concatenate(o_parts, axis=1) * gn_ref[...]
    gate = g_ref[...]
    o_ref[...] = gate * _sigmoid(gate) * o[0:rows, :]
    kv = _bdot((k * kdec_ref[...]).T, v)
    upd = jnp.concatenate([kv[h * HEAD_DIM:(h + 1) * HEAD_DIM, h * RET_DV:(h + 1) * RET_DV]
                           for h in range(N_HEADS)], axis=0)
    st[...] = state * cdec_ref[...] + upd
    sfin_ref[...] = st[...]


def retention(p3, pos0, s0, gn_g):
    b, t_len, _ = p3.shape
    chunk = math.gcd(t_len, RET_CHUNK)
    n_chunk = t_len // chunk
    half = HEAD_DIM // 2
    inv = ROPE_BASE ** (-jnp.arange(half, dtype=F32) / half)
    ang = (pos0 + jnp.arange(t_len)).astype(F32)[:, None] * inv[None, :]
    cos = jnp.tile(jnp.cos(ang), (1, 2 * N_HEADS))
    sin = jnp.tile(jnp.concatenate([-jnp.sin(ang), jnp.sin(ang)], axis=1), (1, N_HEADS))
    log_g = jnp.log(1.0 - 2.0 ** (-5.0 - jnp.arange(N_HEADS, dtype=F32)))
    idx = jnp.arange(RET_CHUNK, dtype=F32)
    rel = idx[:, None] - idx[None, :]
    dmask = jnp.where(rel[None] >= 0, jnp.exp(rel[None] * log_g[:, None, None]), 0.0)
    qdec = jnp.pad(jnp.exp((idx + 1.0)[:, None] * log_g[None, :]), ((0, 0), (0, LANES - N_HEADS)))
    kdec = jnp.where((idx < chunk)[:, None], jnp.exp((chunk - 1.0 - idx)[:, None] * log_g[None, :]), 0.0)
    kdec = jnp.repeat(kdec, HEAD_DIM, axis=1)
    cdec = jnp.broadcast_to(jnp.repeat(jnp.exp(chunk * log_g), HEAD_DIM)[:, None], (HW, RET_DV))
    tok = lambda width, col: pl.BlockSpec((None, chunk, width), lambda i, j: (i, j, col))
    state = pl.BlockSpec((None, HW, RET_DV), lambda i, j: (i, 0, 0))
    return pl.pallas_call(
        functools.partial(_retention_kernel, rows=chunk),
        grid=(b, n_chunk),
        in_specs=[tok(HW, C_RQ // HW), tok(HW, C_RK // HW), tok(RET_V_W, C_RV // RET_V_W),
                  tok(RET_V_W, C_RG // RET_V_W),
                  pl.BlockSpec((chunk, HW), lambda i, j: (j, 0)), pl.BlockSpec((chunk, HW), lambda i, j: (j, 0)),
                  _full((N_HEADS, RET_CHUNK, RET_CHUNK)), _full((RET_CHUNK, LANES)), _full((RET_CHUNK, HW)),
                  _full((HW, RET_DV)), _full((1, RET_V_W)), state],
        out_specs=[pl.BlockSpec((None, chunk, RET_V_W), lambda i, j: (i, j, 0)), state],
        out_shape=[jax.ShapeDtypeStruct((b, t_len, RET_V_W), F32), jax.ShapeDtypeStruct((b, HW, RET_DV), F32)],
        scratch_shapes=[pltpu.VMEM((HW, RET_DV), F32)],
        compiler_params=_cparams("parallel", "arbitrary"),
        name="retention",
    )(p3, p3, p3, p3, cos, sin, dmask, qdec, kdec, cdec, gn_g.reshape(1, RET_V_W), s0)


def _merge_kernel(x_ref, ra_ref, bv_ref, g_ref, ob_ref, oc_ref, ga_ref, gb_ref, gc_ref,
                  gt1_ref, sc2_ref, sh2_ref, lng_ref, lnb_ref, e_ref, wa_ref, wb_ref, wc_ref, wo_ref,
                  n2_ref, rw_ref, rb_ref, x1_o, h2_o, idx_o, gate_o):
    seg_mean = e_ref[...]
    ra = ra_ref[...]
    mu = _hdot(ra, seg_mean)
    dev = ra - mu
    var = _hdot(dev * dev, seg_mean)
    oa = (dev * lax.rsqrt(var + RWKV_GN_EPS) * lng_ref[...] + lnb_ref[...] + bv_ref[...]) * g_ref[...]
    merged = (_sigmoid(ga_ref[...]) * _bdot(oa, wa_ref[...])
              + _sigmoid(gb_ref[...]) * _bdot(ob_ref[...], wb_ref[...])
              + _sigmoid(gc_ref[...]) * _bdot(oc_ref[...], wc_ref[...]))
    x1 = x_ref[...] + gt1_ref[...] * _bdot(merged, wo_ref[...])
    x1_o[...] = x1
    y = x1 * lax.rsqrt(jnp.mean(x1 * x1, axis=-1, keepdims=True) + RMS_EPS) * n2_ref[...]
    h2 = y * (1.0 + sc2_ref[...]) + sh2_ref[...]
    h2_o[...] = h2
    logits = _bdot(h2, rw_ref[...]) + rb_ref[...]
    lane = lax.broadcasted_iota(I32, logits.shape, 1).astype(F32)
    idx_acc = jnp.zeros(logits.shape, F32)
    val_acc = jnp.zeros(logits.shape, F32)
    top = None
    denom = jnp.zeros((logits.shape[0], 1), F32)
    for k in range(TOP_K):
        m = jnp.max(logits, axis=-1, keepdims=True)
        ix = jnp.min(jnp.where(logits == m, lane, float(LANES)), axis=-1, keepdims=True)
        top = m if top is None else top
        ev = jnp.exp(m - top)
        denom = denom + ev
        idx_acc = jnp.where(lane == k, ix, idx_acc)
        val_acc = jnp.where(lane == k, ev, val_acc)
        logits = jnp.where(lane == ix, -jnp.inf, logits)
    idx_o[...] = idx_acc.astype(I32)
    gate_o[...] = val_acc / denom


def merge(x2, t_len, p2, ra, bv, g, ob, oc, gt1, sc2, sh2, lw):
    n, d = x2.shape
    tm = min(n, 256)
    mods = [_mod_rows(m, t_len, tm) for m in (gt1, sc2, sh2)]
    row = lambda width, col=0: pl.BlockSpec((tm, width), lambda i: (i, col))
    gate_col = C_GATE // d
    out_shape = [jax.ShapeDtypeStruct((n, d), F32), jax.ShapeDtypeStruct((n, d), F32),
                 jax.ShapeDtypeStruct((n, LANES), I32), jax.ShapeDtypeStruct((n, LANES), F32)]
    return pl.pallas_call(
        _merge_kernel,
        grid=(n // tm,),
        in_specs=[row(d), row(HW), row(HW), row(HW), row(HW), row(RET_V_W),
                  row(d, gate_col), row(d, gate_col + 1), row(d, gate_col + 2)]
                 + [m[1] for m in mods]
                 + [_full((1, HW)), _full((1, HW)), _full((HW, HW)), _full((HW, d)), _full((HW, d)),
                    _full((RET_V_W, d)), _full((d, d)), _full((1, d)), _full((d, LANES)), _full((1, LANES))],
        out_specs=[row(d), row(d), row(LANES), row(LANES)],
        out_shape=out_shape,
        compiler_params=_cparams("parallel"),
        name="merge",
    )(x2, ra, bv, g, ob, oc, p2, p2, p2, *[m[0] for m in mods],
      lw['ln_g'], lw['ln_b'], lw['seg_mean'], lw['w_up_a'], lw['w_up_b'], lw['w_up_c'], lw['w_out'],
      lw['norm2_g'], lw['router_w'], lw['router_b'])


SPLIT_W = 2 * LANES


def _split_even_odd_kernel(w_ref, perm_ref, even_o, odd_o):
    perm = perm_ref[...]
    for c in range(w_ref.shape[1] // SPLIT_W):
        blk = jnp.dot(w_ref[:, c * SPLIT_W:(c + 1) * SPLIT_W].astype(BF16), perm, preferred_element_type=F32)
        even_o[:, c * LANES:(c + 1) * LANES] = blk[:, 0:LANES].astype(BF16)
        odd_o[:, c * LANES:(c + 1) * LANES] = blk[:, LANES:SPLIT_W].astype(BF16)


def split_even_odd(w):
    n_e, d, f2 = w.shape
    tk = 512
    m = np.arange(LANES)
    perm = np.zeros((SPLIT_W, SPLIT_W), np.float32)
    perm[2 * m, m] = 1.0
    perm[2 * m + 1, LANES + m] = 1.0
    out = jax.ShapeDtypeStruct((n_e, d, f2 // 2), BF16)
    return pl.pallas_call(
        _split_even_odd_kernel,
        grid=(n_e, d // tk),
        in_specs=[pl.BlockSpec((None, tk, f2), lambda e, i: (e, i, 0)), _full((SPLIT_W, SPLIT_W))],
        out_specs=[pl.BlockSpec((None, tk, f2 // 2), lambda e, i: (e, i, 0))] * 2,
        out_shape=[out, out],
        compiler_params=_cparams("parallel", "parallel"),
        name="split_even_odd",
    )(w, jnp.asarray(perm).astype(BF16))


def _moe_ffn_kernel(be_ref, used_ref, tok_ref, tok_next_ref, dst_ref, h_hbm,
                    w1g_ref, w1l_ref, b1g_ref, b1l_ref, w2_ref, b2_ref, y_hbm, xbuf, obuf, in_sem, out_sem):
    i = pl.program_id(0)
    n_steps = pl.num_programs(0)
    rows = xbuf.shape[1]
    slot = i % 2

    def active(j):
        return j * rows < used_ref[0]

    def gather_copy(tok_r, s, r):
        return pltpu.make_async_copy(h_hbm.at[pl.ds(tok_r[0, r], 1)], xbuf.at[s, pl.ds(r, 1)], in_sem.at[s])

    def scatter_copy(s, r):
        return pltpu.make_async_copy(obuf.at[s, pl.ds(r, 1)], y_hbm.at[pl.ds(dst_ref[0, r], 1)], out_sem.at[s])

    def start_all(make):
        def body(r, carry):
            make(r).start()
            return carry
        lax.fori_loop(0, rows, body, 0, unroll=8)

    @pl.when(jnp.logical_and(i == 0, active(0)))
    def _():
        start_all(lambda r: gather_copy(tok_ref, 0, r))

    def wait_gather(s):
        pltpu.make_async_copy(h_hbm.at[pl.ds(0, rows)], xbuf.at[s], in_sem.at[s]).wait()

    def wait_scatter(s):
        pltpu.make_async_copy(obuf.at[s], y_hbm.at[pl.ds(0, rows)], out_sem.at[s]).wait()

    @pl.when(jnp.logical_and(i >= 1, active(i - 1)))
    def _():
        wait_scatter(1 - slot)

        @pl.when(jnp.logical_not(active(i)))
        def _():
            wait_gather(slot)

    @pl.when(active(i))
    def _():
        wait_gather(slot)
        for r in range(rows):
            gather_copy(tok_next_ref, 1 - slot, r).start()
        x = xbuf[slot].astype(BF16)
        glu = jnp.minimum(jnp.dot(x, w1g_ref[...], preferred_element_type=F32) + b1g_ref[...], SWIGLU_LIMIT)
        lin = jnp.clip(jnp.dot(x, w1l_ref[...], preferred_element_type=F32) + b1l_ref[...],
                       -SWIGLU_LIMIT, SWIGLU_LIMIT)
        act = glu * _sigmoid(SWIGLU_ALPHA * glu) * (lin + 1.0)
        obuf[slot] = _bdot(act, w2_ref[...]) + b2_ref[...]
        for r in range(rows):
            scatter_copy(slot, r).start()

        @pl.when(i == n_steps - 1)
        def _():
            wait_scatter(slot)
            wait_gather(1 - slot)


def moe_ffn(block_expert, n_used, row_tok, row_dst, h2, lw, rows, n_out):
    n_blocks = row_tok.shape[0]
    d = h2.shape[1]
    d_ff = lw['moe_w2'].shape[1]
    wspec = lambda a, b_: pl.BlockSpec((None, a, b_), lambda i, be, nu: (be[i], 0, 0))
    idx_spec = lambda shift: pl.BlockSpec((None, 1, rows), lambda i, be, nu: (jnp.minimum(i + shift, n_blocks - 1), 0, 0),
                                          memory_space=pltpu.SMEM)
    grid_spec = pltpu.PrefetchScalarGridSpec(
        num_scalar_prefetch=2,
        grid=(n_blocks,),
        in_specs=[idx_spec(0), idx_spec(1), idx_spec(0), pl.BlockSpec(memory_space=pl.ANY),
                  wspec(d, d_ff), wspec(d, d_ff), wspec(1, d_ff), wspec(1, d_ff), wspec(d_ff, d), wspec(1, d)],
        out_specs=pl.BlockSpec(memory_space=pl.ANY),
        scratch_shapes=[pltpu.VMEM((2, rows, d), F32), pltpu.VMEM((2, rows, d), F32),
                        pltpu.SemaphoreType.DMA((2,)), pltpu.SemaphoreType.DMA((2,))])
    return pl.pallas_call(
        _moe_ffn_kernel,
        grid_spec=grid_spec,
        out_shape=jax.ShapeDtypeStruct((n_out, d), F32),
        compiler_params=pltpu.CompilerParams(dimension_semantics=("arbitrary",), vmem_limit_bytes=VMEM_LIMIT,
                                             disable_bounds_checks=True),
        name="moe_ffn",
    )(block_expert, n_used, row_tok, row_tok, row_dst, h2, lw['moe_w1g'], lw['moe_w1l'], lw['moe_b1g'],
      lw['moe_b1l'], lw['moe_w2'], lw['moe_b2'])


def _combine_kernel(x_ref, y0_ref, y1_ref, y2_ref, y3_ref, gate_ref, gt2_ref, o_ref):
    gate = gate_ref[...].astype(BF16).astype(F32)
    y = jnp.zeros(x_ref.shape, F32)
    for k, y_ref in enumerate((y0_ref, y1_ref, y2_ref, y3_ref)):
        y = y + gate[:, k:k + 1] * y_ref[...].astype(BF16).astype(F32)
    o_ref[...] = x_ref[...] + gt2_ref[...] * y


def combine(x1, t_len, y_slots, gate, gt2):
    n, d = x1.shape
    tm = min(n, 256)
    gt2_3, gt2_spec = _mod_rows(gt2, t_len, tm)
    slot_spec = lambda k: pl.BlockSpec((tm, d), lambda i: (k * (n // tm) + i, 0))
    return pl.pallas_call(
        _combine_kernel,
        grid=(n // tm,),
        in_specs=[pl.BlockSpec((tm, d), lambda i: (i, 0))] + [slot_spec(k) for k in range(TOP_K)]
                 + [pl.BlockSpec((tm, LANES), lambda i: (i, 0)), gt2_spec],
        out_specs=pl.BlockSpec((tm, d), lambda i: (i, 0)),
        out_shape=jax.ShapeDtypeStruct((n, d), F32),
        compiler_params=_cparams("parallel"),
        name="combine",
    )(x1, *([y_slots] * TOP_K), gate, gt2_3)


def moe(x1, t_len, h2, top_idx, gate, gt2, lw):
    n, d = h2.shape
    rows = 256 if n * TOP_K >= 8192 else 64
    a_n = n * TOP_K
    e_flat = top_idx[:, :TOP_K].reshape(-1)
    onehot = (e_flat[:, None] == jnp.arange(N_EXPERTS, dtype=I32)[None, :]).astype(I32)
    ranks = jnp.cumsum(onehot, axis=0) - onehot
    rank = jnp.sum(ranks * onehot, axis=1)
    counts = jnp.sum(onehot, axis=0)
    padded = (counts + rows - 1) // rows * rows
    pends = jnp.cumsum(padded)
    pstarts = pends - padded
    dest = (pstarts[e_flat] + rank).astype(I32)
    n_blocks = -(-a_n // rows) + N_EXPERTS
    n_rows = n_blocks * rows
    slot_row = (jnp.arange(TOP_K, dtype=I32)[None, :] * n + jnp.arange(n, dtype=I32)[:, None]).reshape(-1)
    row_dst = (a_n + jnp.arange(n_rows, dtype=I32)).at[dest].set(slot_row)
    row_tok = jnp.where(row_dst < a_n, row_dst % n, 0)
    block_start = jnp.arange(n_blocks, dtype=I32) * rows
    block_expert = jnp.minimum(jnp.sum((pends[None, :] <= block_start[:, None]).astype(I32), axis=1),
                               N_EXPERTS - 1).astype(I32)
    y_slots = moe_ffn(block_expert, pends[-1:].astype(I32), row_tok.reshape(n_blocks, 1, rows),
                      row_dst.reshape(n_blocks, 1, rows), h2, lw, rows, a_n + n_rows)
    return combine(x1, t_len, y_slots, gate, gt2)


def _t5_bucket(dist):
    max_exact = REL_BUCKETS // 2
    d = jnp.maximum(dist, 0)
    large = max_exact + (jnp.log(jnp.maximum(d, 1).astype(F32) / max_exact)
                         / math.log(REL_MAX_DIST / max_exact) * (REL_BUCKETS - max_exact)).astype(I32)
    large = jnp.minimum(large, REL_BUCKETS - 1)
    return jnp.where(d < max_exact, d, large)


def _seg_matrix(value):
    head = np.arange(HW) // HEAD_DIM
    return jnp.asarray((head[:, None] == head[None, :]).astype(np.float32) * value)


def _pack_layer_weights(l, w):
    d = D_MODEL
    parts = jnp.split(w['w_in'][l], np.cumsum(np.array(IN_WIDTHS))[:-1].tolist(), axis=1)
    p_rwkv, pq, pk, pv, pqi, pki, pwi, rq, rk, rv, rg, pgate = parts
    zeros = lambda n: jnp.zeros((d, n), F32)
    w_in = jnp.concatenate([p_rwkv, zeros(RWKV_PAD_W - RWKV_PROJ_W), pki, pwi, zeros(LANES - HEAD_DIM - N_HEADS),
                            pq, pk, pv, pqi, rq, rk, rv, rg, pgate], axis=1).astype(BF16)
    lora = jnp.zeros((3, LORA_W, HW), F32)
    lora = lora.at[0, 0:DECAY_LORA].set(w['rwkv_w2'][l])
    lora = lora.at[1, DECAY_LORA:DECAY_LORA + AAA_LORA].set(w['rwkv_a2'][l])
    lora = lora.at[2, DECAY_LORA + AAA_LORA:DECAY_LORA + AAA_LORA + GATE_LORA].set(w['rwkv_g2'][l])
    vecs = jnp.zeros((8, HW), F32)
    for i, name in enumerate(('rwkv_w0', 'rwkv_a0', 'rwkv_k_k', 'rwkv_k_a', 'rwkv_r_k')):
        vecs = vecs.at[i].set(w[name][l])
    pad_lanes = lambda v, fill=0.0: jnp.concatenate(
        [v, jnp.full((LANES - v.shape[0],), fill, F32)]).reshape(1, LANES)
    b1 = w['moe_b1'][l]
    return dict(
        w_in=w_in, norm1_g=w['norm1_g'][l],
        mu=jnp.pad(w['rwkv_mu'][l], (0, RWKV_PAD_W - RWKV_PROJ_W)).reshape(1, RWKV_PAD_W),
        lora=lora.astype(BF16), vecs=vecs,
        qg=jnp.tile(w['q_norm_g'][l], N_HEADS).reshape(1, HW), kg=jnp.tile(w['k_norm_g'][l], N_HEADS).reshape(1, HW),
        ig=pad_lanes(w['idx_k_norm_g'][l]), ib=pad_lanes(w['idx_k_norm_b'][l]),
        ret_gn_g=w['ret_gn_g'][l],
        ln_g=w['rwkv_ln_g'][l].reshape(1, HW), ln_b=w['rwkv_ln_b'][l].reshape(1, HW),
        w_up_a=w['w_up_rwkv'][l].astype(BF16), w_up_b=w['w_up_att'][l].astype(BF16),
        w_up_c=w['w_up_ret'][l].astype(BF16), w_out=w['w_out'][l].astype(BF16),
        norm2_g=w['norm2_g'][l].reshape(1, d),
        router_w=jnp.pad(w['router_w'][l], ((0, 0), (0, LANES - N_EXPERTS))),
        router_b=pad_lanes(w['router_b'][l], NEG_INF),
        moe_w1g=w['moe_w1g'][l], moe_w1l=w['moe_w1l'][l],
        moe_b1g=b1[:, None, 0::2], moe_b1l=b1[:, None, 1::2],
        moe_w2=w['moe_w2'][l].astype(BF16), moe_b2=w['moe_b2'][l][:, None, :],
    )


def _attend_prompt(p3, qn, kn, kiw, rel_bias, tri):
    b, s_len, _ = p3.shape
    qb = 128
    nb = s_len // qb
    n_sel = max(1, min(TOPK_KEYS, s_len // 4))
    scale = HEAD_DIM ** -0.5
    qi = p3[:, :, C_QI:C_QI + HW].reshape(b, nb, qb, N_HEADS, HEAD_DIM).transpose(0, 1, 3, 2, 4)
    qi_stack = (qi * scale).reshape(b, nb, N_HEADS * qb, HEAD_DIM).astype(BF16)
    kiw3 = kiw.reshape(b, s_len, LANES)
    mask = dsa_select(qi_stack, kiw3[:, :, :HEAD_DIM].astype(BF16), kiw3, tri, n_sel)
    r = jnp.arange(qb)[:, None]
    c = jnp.arange(ATT_NEAR)[None, :]
    dist = jnp.stack([r - c, qb + r - c])
    toe = rel_bias[_t5_bucket(dist)].transpose(0, 3, 1, 2)
    toe = toe - rel_bias[REL_BUCKETS - 1][None, :, None, None]
    return dsa_attn((qn.reshape(b, s_len, HW) * scale).astype(BF16), kn.reshape(b, s_len, HW).astype(BF16),
                    p3[:, :, C_V:C_V + HW].astype(BF16), mask, toe)


def _attend_sample(p3, qn, kn, kiw, rel_bias, tri, cache_k_t, cache_v_t, cache_ki_t, layer, page_table):
    db, t_new, _ = p3.shape
    page = cache_ki_t.shape[3]
    n_pages = page_table.shape[1]
    assert page == LANES and page >= REL_MAX_DIST and t_new <= page
    n_sel = max(1, min(TOPK_KEYS, (n_pages * page + t_new) // 4))
    scale = HEAD_DIM ** -0.5
    rows = N_HEADS * t_new
    qi = p3[:, :, C_QI:C_QI + HW].reshape(db, t_new, N_HEADS, HEAD_DIM).transpose(0, 2, 1, 3)
    qi_rows = (qi * scale).reshape(db, rows, HEAD_DIM).astype(BF16)
    kiw3 = kiw.reshape(db, t_new, LANES)
    wcol = kiw3[:, :, HEAD_DIM:HEAD_DIM + N_HEADS].transpose(0, 2, 1).reshape(db, rows, 1)
    wcol = jnp.broadcast_to(wcol, (db, rows, LANES))
    new_t = lambda a: jnp.pad(a, ((0, 0), (0, page - t_new), (0, 0))).transpose(0, 2, 1)
    t_of_row = jnp.tile(jnp.arange(t_new), N_HEADS)[:, None]
    j = jnp.arange(page)[None, :]
    dist = jnp.stack([page + t_of_row - j, t_of_row - j])
    h_of_row = jnp.repeat(jnp.arange(N_HEADS), t_new)
    bias_tab = (rel_bias[_t5_bucket(dist), h_of_row[None, :, None]]
                - rel_bias[REL_BUCKETS - 1][h_of_row][None, :, None])
    return dsa_sample(page_table, layer, qi_rows, wcol, qn.reshape(db, t_new, HW) * scale,
                      cache_ki_t, new_t(kiw3[:, :, :HEAD_DIM]),
                      cache_k_t, new_t(kn.reshape(db, t_new, HW)),
                      cache_v_t, new_t(p3[:, :, C_V:C_V + HW]),
                      bias_tab, tri, n_sel)


def _layer_group(x, mods, lw, consts, rel_bias, pos0, shift0, rwkv0, ret0, attend):
    b, t_len, d = x.shape
    n = b * t_len
    sh1, sc1, gt1, sh2, sc2, gt2 = mods
    x2 = x.reshape(n, d)
    p2 = norm_proj(x2, t_len, lw['norm1_g'], sc1, sh1, lw['w_in'])
    p3 = p2.reshape(b, t_len, PROJ_W)
    prev = jnp.pad(shift0, ((0, 0), (0, RWKV_PAD_W - RWKV_PROJ_W)))[:, None, :]
    r, w, k2, v, kk, kka, g, bv, shift_new = rwkv_prep(p3, prev, lw['mu'], lw['vecs'], lw['lora'],
                                                       consts['seg_ones'])
    tp = -(-t_len // RWKV_CHUNK) * RWKV_CHUNK
    pad_t = lambda a: a if tp == t_len else jnp.pad(a, ((0, 0), (0, tp - t_len), (0, 0)))
    s0 = rwkv0.reshape(b, HW // LANES, 2, HEAD_DIM, HEAD_DIM)
    zero = jnp.zeros_like(s0[:, :, 0])
    s0_bd = jnp.concatenate([jnp.concatenate([s0[:, :, 0], zero], axis=-1),
                             jnp.concatenate([zero, s0[:, :, 1]], axis=-1)], axis=-2)
    ra, s_fin = rwkv_chunked(pad_t(w), pad_t(kk), pad_t(kka), pad_t(k2), pad_t(r), pad_t(v), s0_bd)
    ra = ra[:, :t_len]
    rwkv_new = jnp.stack([s_fin[:, :, :HEAD_DIM, :HEAD_DIM], s_fin[:, :, HEAD_DIM:, HEAD_DIM:]],
                         axis=2).reshape(b, N_HEADS, HEAD_DIM, HEAD_DIM)
    qn, kn, kiw = dsa_prep(p2, lw['qg'], lw['kg'], lw['ig'], lw['ib'], consts['seg_mean'])
    ob = attend(p3, qn, kn, kiw, rel_bias, consts['tri'])
    oc, ret_new = retention(p3, pos0, ret0.reshape(b, HW, RET_DV), lw['ret_gn_g'])
    lw = dict(lw, seg_mean=consts['seg_mean'])
    x1, h2, top_idx, gate = merge(x2, t_len, p2, ra.reshape(n, HW), bv.reshape(n, HW), g.reshape(n, HW),
                                  ob.reshape(n, HW), oc.reshape(n, RET_V_W), gt1, sc2, sh2, lw)
    x_out = moe(x1, t_len, h2, top_idx, gate, gt2, lw)
    return (x_out.reshape(b, t_len, d),
            kn.reshape(b, t_len, N_HEADS, HEAD_DIM),
            p3[:, :, C_V:C_V + HW].reshape(b, t_len, N_HEADS, HEAD_DIM),
            kiw.reshape(b, t_len, LANES)[:, :, :HEAD_DIM],
            rwkv_new,
            shift_new[:, 0, :RWKV_PROJ_W],
            ret_new.reshape(b, N_HEADS, HEAD_DIM, RET_DV))


def kernel(x_prompt, x_sample, c_prompt, c_sample, cache_k, cache_v, cache_kidx, state_rwkv, state_rwkv_shift, state_ret, page_table, rel_bias, ada_w, ada_b, norm1_g, norm2_g, w_in, rwkv_mu, rwkv_w0, rwkv_w2, rwkv_a0, rwkv_a2, rwkv_g2, rwkv_k_k, rwkv_k_a, rwkv_r_k, rwkv_ln_g, rwkv_ln_b, q_norm_g, k_norm_g, idx_k_norm_g, idx_k_norm_b, ret_gn_g, w_up_rwkv, w_up_att, w_up_ret, w_out, router_w, router_b, moe_w1, moe_b1, moe_w2, moe_b2):
    weights = dict(norm1_g=norm1_g, norm2_g=norm2_g, w_in=w_in, rwkv_mu=rwkv_mu, rwkv_w0=rwkv_w0,
                   rwkv_w2=rwkv_w2, rwkv_a0=rwkv_a0, rwkv_a2=rwkv_a2, rwkv_g2=rwkv_g2, rwkv_k_k=rwkv_k_k,
                   rwkv_k_a=rwkv_k_a, rwkv_r_k=rwkv_r_k, rwkv_ln_g=rwkv_ln_g, rwkv_ln_b=rwkv_ln_b,
                   q_norm_g=q_norm_g, k_norm_g=k_norm_g, idx_k_norm_g=idx_k_norm_g,
                   idx_k_norm_b=idx_k_norm_b, ret_gn_g=ret_gn_g, w_up_rwkv=w_up_rwkv, w_up_att=w_up_att,
                   w_up_ret=w_up_ret, w_out=w_out, router_w=router_w, router_b=router_b,
                   moe_b1=moe_b1, moe_w2=moe_w2, moe_b2=moe_b2)
    depth = w_in.shape[0]
    w1_shape = moe_w1.shape
    w1g, w1l = split_even_odd(moe_w1.reshape((-1,) + w1_shape[2:]))
    weights['moe_w1g'] = w1g.reshape(w1_shape[:3] + (w1_shape[3] // 2,))
    weights['moe_w1l'] = w1l.reshape(w1_shape[:3] + (w1_shape[3] // 2,))
    bp, dbatch = x_prompt.shape[0], x_sample.shape[0]
    past = page_table.shape[1] * cache_k.shape[2]
    dt = x_prompt.dtype
    consts = dict(seg_ones=_seg_matrix(1.0), seg_mean=_seg_matrix(1.0 / HEAD_DIM),
                  tri=jnp.asarray(np.triu(np.ones((LANES, LANES), np.float32), 1)).astype(BF16))
    n_l, n_pool, page = cache_k.shape[:3]
    cache_k_t = cache_k.transpose(0, 1, 3, 4, 2).reshape(n_l, n_pool, HW, page)
    cache_v_t = cache_v.transpose(0, 1, 3, 4, 2).reshape(n_l, n_pool, HW, page)
    cache_ki_t = cache_kidx.transpose(0, 1, 3, 2)
    n_c = bp + dbatch
    c_all = jnp.pad(jnp.concatenate([c_prompt, c_sample], axis=0), ((0, -n_c % 8), (0, 0)))
    mods_all = ada_all(c_all, ada_w, ada_b)
    xp, xs = x_prompt, x_sample
    new_p = [[] for _ in range(6)]
    new_s = [[] for _ in range(6)]
    for l in range(depth):
        lw = _pack_layer_weights(l, weights)
        mods_p = jnp.split(mods_all[l, :bp], 6, axis=-1)
        mods_s = jnp.split(mods_all[l, bp:n_c], 6, axis=-1)
        xp, *st_p = _layer_group(xp, mods_p, lw, consts, rel_bias, 0,
                                 jnp.zeros((bp, RWKV_PROJ_W), dt),
                                 jnp.zeros((bp, N_HEADS, HEAD_DIM, HEAD_DIM), dt),
                                 jnp.zeros((bp, N_HEADS, HEAD_DIM, RET_DV), dt),
                                 _attend_prompt)
        attend_s = functools.partial(_attend_sample, cache_k_t=cache_k_t, cache_v_t=cache_v_t,
                                     cache_ki_t=cache_ki_t, layer=l, page_table=page_table)
        xs, *st_s = _layer_group(xs, mods_s, lw, consts, rel_bias, past, state_rwkv_shift[l], state_rwkv[l],
                                 state_ret[l], attend_s)
        for lst, a in zip(new_p, st_p):
            lst.append(a)
        for lst, a in zip(new_s, st_s):
            lst.append(a)
    return (xp, xs, *[jnp.stack(a) for a in new_p], *[jnp.stack(a) for a in new_s])
```
